```python
import math
import jax, jax.numpy as jnp
from jax import lax
import numpy as np

D_MODEL = 1024
BATCH = 32
SEQ = 256
DEPTH = 2
DEC_BATCH = 8
DEC_SEQ = 4096
PAST_LEN = 512

GRID_W = 64
N_EVEN = (DEPTH + 1) // 2
N_ODD = DEPTH // 2
RET_HEADS = 4
RET_DK = 128
RET_DV = 128
RET_WIDTH = RET_HEADS * RET_DV
RET_CHUNK = 128
ROPE_BASE = 10000.0
CONV_WIDTH = D_MODEL // 2
CONV_K = 3
AB_IN = 3 * RET_HEADS * RET_DK + RET_WIDTH + 3 * CONV_WIDTH
AB_IN = 2 * RET_HEADS * RET_DK + 2 * RET_WIDTH + 3 * CONV_WIDTH
AB_OUT = RET_WIDTH + CONV_WIDTH
CMLP_WIDTH = D_MODEL
CMLP_GROUPS = 4
CMLP_CHUNK = 128
FFN_HIDDEN = ((8 * D_MODEL // 3 + 255) // 256) * 256
EPS = 1e-6

kernel_name = 'hybrid_retention_conv_chunkmlp_diffusion_step'


def _rmsnorm(x, g):
    x32 = x.astype(jnp.float32)
    y = x32 * lax.rsqrt(jnp.mean(x32 * x32, axis=-1, keepdims=True) + EPS)
    return (y * g.astype(jnp.float32)).astype(x.dtype)


def _rope_2d(T):
    rows = T // GRID_W
    row = jnp.repeat(jnp.arange(rows, dtype=jnp.float32), GRID_W)
    col = jnp.tile(jnp.arange(GRID_W, dtype=jnp.float32), rows)
    nf = RET_DK // 4
    freqs = ROPE_BASE ** (-jnp.arange(nf, dtype=jnp.float32) / nf)
    ang = jnp.concatenate([row[:, None] * freqs, col[:, None] * freqs], axis=-1)
    return jnp.cos(ang)[:, None, :], jnp.sin(ang)[:, None, :]


def _apply_rope(x, cos, sin):
    half = x.shape[-1] // 2
    x1, x2 = x[..., :half], x[..., half:]
    cos = cos.astype(x.dtype)
    sin = sin.astype(x.dtype)
    return jnp.concatenate([x1 * cos - x2 * sin, x2 * cos + x1 * sin], axis=-1)


def _retention_scan(q, k, v, log_gamma, S0):
    B, T, H, DK = q.shape
    DV = v.shape[-1]
    n = T // RET_CHUNK
    dt = q.dtype

    def to_chunks(a):
        return a.reshape(B, n, RET_CHUNK, H, a.shape[-1]).transpose(1, 0, 3, 2, 4)

    qc, kc, vc = to_chunks(q), to_chunks(k), to_chunks(v)
    idx = jnp.arange(RET_CHUNK, dtype=jnp.float32)
    diff = idx[:, None] - idx[None, :]
    lg = log_gamma[:, None, None]
    dmask = jnp.where(diff >= 0, jnp.exp(jnp.maximum(diff, 0.0) * lg), 0.0).astype(dt)
    cross = jnp.exp((idx + 1.0) * log_gamma[:, None]).astype(dt)[..., None]
    kdec = jnp.exp((RET_CHUNK - 1.0 - idx) * log_gamma[:, None]).astype(dt)[..., None]
    chunk_dec = jnp.exp(RET_CHUNK * log_gamma).astype(dt)[:, None, None]

    def step(S, inp):
        qb, kb, vb = inp
        scores = jnp.einsum('bhid,bhjd->bhij', qb, kb) * dmask
        o = jnp.einsum('bhij,bhjv->bhiv', scores, vb) + jnp.einsum('bhid,bhdv->bhiv', qb, S) * cross
        S_new = S * chunk_dec + jnp.einsum('bhjd,bhjv->bhdv', kb * kdec, vb)
        return S_new, o

    S_fin, o = lax.scan(step, S0.astype(dt), (qc, kc, vc))
    o = o.transpose(1, 0, 3, 2, 4).reshape(B, T, H, DV)
    return o, S_fin


def _short_conv(x, w, b):
    xp = jnp.pad(x, ((0, 0), (1, 1), (0, 0)))
    return w[0] * xp[:, :-2] + w[1] * xp[:, 1:-1] + w[2] * xp[:, 2:] + b


def _mixer_ab(h, w_in, decay_logit, ret_g, conv_w, conv_b, w_out, S0, use_rope):
    B, T, _ = h.shape
    p = h @ w_in
    qd = RET_HEADS * RET_DK
    splits = np.cumsum([qd, qd, RET_WIDTH, RET_WIDTH, CONV_WIDTH, CONV_WIDTH])
    q, k, v, g, bg, cg, xc = jnp.split(p, splits, axis=-1)
    q = q.reshape(B, T, RET_HEADS, RET_DK) * (RET_DK ** -0.5)
    k = k.reshape(B, T, RET_HEADS, RET_DK)
    v = v.reshape(B, T, RET_HEADS, RET_DV)
    if use_rope:
        cos, sin = _rope_2d(T)
        q = _apply_rope(q, cos, sin)
        k = _apply_rope(k, cos, sin)
    lg = jax.nn.log_sigmoid(decay_logit.astype(jnp.float32))
    o_f, S_f = _retention_scan(q, k, v, lg[0], S0[:, 0])
    o_b, S_b = _retention_scan(q[:, ::-1], k[:, ::-1], v[:, ::-1], lg[1], S0[:, 1])
    o = o_f + o_b[:, ::-1]
    o = _rmsnorm(o, jnp.ones((RET_DV,), jnp.float32)).reshape(B, T, RET_WIDTH) * ret_g
    ya = jax.nn.silu(g) * o
    yb = bg * _short_conv(cg * xc, conv_w, conv_b)
    out = jnp.concatenate([ya, yb], axis=-1) @ w_out
    S_fin = jnp.stack([S_f, S_b], axis=1)
    return out, S_fin


def _mixer_c(h, w_in, v_g, w_s, b_s, w_out):
    B, T, _ = h.shape
    z = jax.nn.gelu(h @ w_in)
    u, v = jnp.split(z, 2, axis=-1)
    v = _rmsnorm(v, v_g)
    n = T // CMLP_CHUNK
    v = v.reshape(B, n, CMLP_CHUNK, CMLP_GROUPS, CMLP_WIDTH // CMLP_GROUPS)
    s = jnp.einsum('gpq,bnqgc->bnpgc', w_s, v) + b_s.T[None, None, :, :, None]
    return (u * s.reshape(B, T, CMLP_WIDTH)) @ w_out


def _swiglu(h, w_gate, w_up, w_down):
    return (jax.nn.silu(h @ w_gate) * (h @ w_up)) @ w_down


def _trunk(x, cvec, ret_init, use_rope, ada_w, ada_b, norm_mix_g, norm_ffn_g, w_in_ab, ret_decay_logit,
           ret_norm_g, conv_w, conv_b, w_out_ab, w_in_c, c_norm_g, w_spatial, b_spatial, w_out_c,
           w_gate, w_up, w_down, final_norm_g):
    states = []
    sc = jax.nn.silu(cvec)
    for l in range(DEPTH):
        mod = sc @ ada_w[l] + ada_b[l]
        sh1, sc1, g1, sh2, sc2, g2 = [m[:, None, :] for m in jnp.split(mod, 6, axis=-1)]
        h = _rmsnorm(x, norm_mix_g[l]) * (1.0 + sc1) + sh1
        if l % 2 == 0:
            i = l // 2
            out, S_fin = _mixer_ab(h, w_in_ab[i], ret_decay_logit[i], ret_norm_g[i], conv_w[i], conv_b[i],
                                   w_out_ab[i], ret_init[:, i], use_rope)
            states.append(S_fin)
        else:
            i = l // 2
            out = _mixer_c(h, w_in_c[i], c_norm_g[i], w_spatial[i], b_spatial[i], w_out_c[i])
        x = x + g1 * out
        h = _rmsnorm(x, norm_ffn_g[l]) * (1.0 + sc2) + sh2
        x = x + g2 * _swiglu(h, w_gate[l], w_up[l], w_down[l])
    return _rmsnorm(x, final_norm_g), jnp.stack(states, axis=1)


def setup_inputs(seed: int = 0) -> dict:
    key = jax.random.key(seed)
    ks = jax.random.split(key, 24)
    f32 = jnp.float32
    nrm = lambda k, shape, s: jax.random.normal(k, shape, f32) * s
    gamma = 1.0 - 2.0 ** (-5.0 - np.arange(RET_HEADS))
    base_logit = jnp.asarray(np.log(gamma / (1.0 - gamma)), f32)
    return {
        'x_prompt': nrm(ks[0], (BATCH, SEQ, D_MODEL), 1.0),
        'x_sample': nrm(ks[1], (DEC_BATCH, DEC_SEQ, D_MODEL), 1.0),
        'state_ret': nrm(ks[2], (DEC_BATCH, N_EVEN, 2, RET_HEADS, RET_DK, RET_DV), 0.5),
        'c': nrm(ks[3], (DEC_BATCH, D_MODEL), 1.0),
        'c_ctx': nrm(ks[4], (D_MODEL,), 1.0),
        'ada_w': nrm(ks[5], (DEPTH, D_MODEL, 6 * D_MODEL), 0.5 * D_MODEL ** -0.5),
        'ada_b': nrm(ks[6], (DEPTH, 6 * D_MODEL), 0.02),
        'norm_mix_g': 1.0 + nrm(ks[7], (DEPTH, D_MODEL), 0.02),
        'norm_ffn_g': 1.0 + nrm(ks[8], (DEPTH, D_MODEL), 0.02),
        'w_in_ab': nrm(ks[9], (N_EVEN, D_MODEL, AB_IN), D_MODEL ** -0.5),
        'ret_decay_logit': base_logit + nrm(ks[10], (N_EVEN, 2, RET_HEADS), 0.1),
        'ret_norm_g': 1.0 + nrm(ks[11], (N_EVEN, RET_WIDTH), 0.02),
        'conv_w': nrm(ks[12], (N_EVEN, CONV_K, CONV_WIDTH), CONV_K ** -0.5),
        'conv_b': nrm(ks[13], (N_EVEN, CONV_WIDTH), 0.02),
        'w_out_ab': nrm(ks[14], (N_EVEN, AB_OUT, D_MODEL), AB_OUT ** -0.5),
        'w_in_c': nrm(ks[15], (N_ODD, D_MODEL, 2 * CMLP_WIDTH), D_MODEL ** -0.5),
        'c_norm_g': 1.0 + nrm(ks[16], (N_ODD, CMLP_WIDTH), 0.02),
        'w_spatial': nrm(ks[17], (N_ODD, CMLP_GROUPS, CMLP_CHUNK, CMLP_CHUNK), CMLP_CHUNK ** -0.5),
        'b_spatial': 1.0 + nrm(ks[18], (N_ODD, CMLP_GROUPS, CMLP_CHUNK), 0.02),
        'w_out_c': nrm(ks[19], (N_ODD, CMLP_WIDTH, D_MODEL), CMLP_WIDTH ** -0.5),
        'w_gate': nrm(ks[20], (DEPTH, D_MODEL, FFN_HIDDEN), D_MODEL ** -0.5),
        'w_up': nrm(ks[21], (DEPTH, D_MODEL, FFN_HIDDEN), D_MODEL ** -0.5),
        'w_down': nrm(ks[22], (DEPTH, FFN_HIDDEN, D_MODEL), FFN_HIDDEN ** -0.5),
        'final_norm_g': 1.0 + nrm(ks[23], (D_MODEL,), 0.02),
    }


def reference(x_prompt, x_sample, state_ret, c, c_ctx, ada_w, ada_b, norm_mix_g, norm_ffn_g, w_in_ab,
              ret_decay_logit, ret_norm_g, conv_w, conv_b, w_out_ab, w_in_c, c_norm_g, w_spatial,
              b_spatial, w_out_c, w_gate, w_up, w_down, final_norm_g):
    weights = (ada_w, ada_b, norm_mix_g, norm_ffn_g, w_in_ab, ret_decay_logit, ret_norm_g, conv_w, conv_b,
               w_out_ab, w_in_c, c_norm_g, w_spatial, b_spatial, w_out_c, w_gate, w_up, w_down, final_norm_g)
    ctx_init = jnp.zeros((x_prompt.shape[0], N_EVEN, 2, RET_HEADS, RET_DK, RET_DV), x_prompt.dtype)
    y_prompt, new_state_ret = _trunk(x_prompt, c_ctx[None, :], ctx_init, False, *weights)
    y_sample, _ = _trunk(x_sample, c, state_ret, True, *weights)
    return (y_prompt, y_sample, new_state_ret)
```

```python
import functools
import math

import jax
import jax.numpy as jnp
from jax import lax
from jax.experimental import pallas as pl
from jax.experimental.pallas import tpu as pltpu

D_MODEL = 1024
N_HEADS = 4
HEAD_DIM = 128
RET_WIDTH = N_HEADS * HEAD_DIM
CONV_WIDTH = 512
CHUNK = 128
CMLP_GROUPS = 4
GROUP_WIDTH = D_MODEL // CMLP_GROUPS
FFN_HIDDEN = 2816
AB_IN = 4 * RET_WIDTH + 3 * CONV_WIDTH
GRID_W = 64
ROPE_BASE = 10000.0
EPS = 1e-6
MOD_ROWS = 16
MOD_COLS_PER_STEP = 1536

P_Q, P_K, P_V, P_G, P_BG, P_CX = 0, 512, 1024, 1536, 2048, 2560
P_COLS = 3072
HALO_ROWS = 16

TOKEN_TILE = 256
VMEM_LIMIT_BYTES = 56 * 1024 * 1024

F32 = jnp.float32
BF16 = jnp.bfloat16


def _dot(a, b):
    return jnp.dot(a, b, preferred_element_type=F32)


def _resident(shape):
    zeros = (0,) * len(shape)
    return pl.BlockSpec(shape, lambda *_: zeros, pipeline_mode=pl.Buffered(1))


def _norm_mod(x, gain, shift, scale):
    y = x * lax.rsqrt(jnp.mean(x * x, axis=-1, keepdims=True) + EPS)
    return (y * gain) * (1.0 + scale) + shift


def _silu(x):
    return x * jax.nn.sigmoid(x)


def _gelu_tanh(x):
    return 0.5 * x * (1.0 + jnp.tanh(math.sqrt(2.0 / math.pi) * (x + 0.044715 * (x * x * x))))


def _ffn_residual(x, mod, gain, wg_ref, wu_ref, wd_ref):
    h = _norm_mod(x, gain, mod[3:4], mod[4:5]).astype(BF16)
    act = (_silu(_dot(h, wg_ref[...])) * _dot(h, wu_ref[...])).astype(BF16)
    return x + mod[5:6] * _dot(act, wd_ref[...])


def _mod_kernel(cv_ref, w_ref, b_ref, o_ref):
    sc = _silu(cv_ref[...]).astype(BF16)
    o_ref[...] = _dot(sc, w_ref[...].astype(BF16)) + b_ref[...]


def _modulation(cvecs, ada_w, ada_b):
    depth, _, n_mod = ada_w.shape
    steps = n_mod // MOD_COLS_PER_STEP
    out = pl.pallas_call(
        _mod_kernel,
        grid=(depth, steps),
        in_specs=[
            pl.BlockSpec((MOD_ROWS, D_MODEL), lambda l, n: (0, 0)),
            pl.BlockSpec((None, D_MODEL, MOD_COLS_PER_STEP), lambda l, n: (l, 0, n)),
            pl.BlockSpec((None, 1, MOD_COLS_PER_STEP), lambda l, n: (l, 0, n)),
        ],
        out_specs=pl.BlockSpec((None, MOD_ROWS, MOD_COLS_PER_STEP), lambda l, n: (l, 0, n)),
        out_shape=jax.ShapeDtypeStruct((depth, MOD_ROWS, n_mod), F32),
        compiler_params=pltpu.CompilerParams(
            dimension_semantics=("arbitrary", "arbitrary"), vmem_limit_bytes=VMEM_LIMIT_BYTES),
        name="adaln_modulation",
    )(cvecs, ada_w, ada_b.reshape(depth, 1, n_mod))
    return out.reshape(depth, MOD_ROWS, 6, D_MODEL)


def _inproj_kernel(*refs, n_chunks, use_rope, has_init):
    it = iter(refs)
    x_ref, mod_ref, gain_ref, w_ref = next(it), next(it), next(it), next(it)
    cos_ref = sin_ref = s0_ref = None
    if use_rope:
        cos_ref, sin_ref = next(it), next(it)
    kdf_ref, kdb_ref, cdb_ref = next(it), next(it), next(it)
    if has_init:
        s0_ref = next(it)
    p_ref, dsf_ref, sbp_ref, sb_ref = next(it), next(it), next(it), next(it)

    @pl.when(pl.program_id(1) == 0)
    def _():
        if has_init:
            sb_ref[...] = s0_ref[...]
        else:
            sb_ref[...] = jnp.zeros_like(sb_ref)

    mod = mod_ref[...]
    h = _norm_mod(x_ref[...], gain_ref[...], mod[0:1], mod[1:2]).astype(BF16)
    p = _dot(h, w_ref[...])

    q = p[:, 0:RET_WIDTH] * (HEAD_DIM ** -0.5)
    k = p[:, RET_WIDTH:2 * RET_WIDTH]
    v = p[:, 2 * RET_WIDTH:3 * RET_WIDTH]
    if use_rope:
        cos2, sin2 = cos_ref[...], sin_ref[...]

        def rope(a):
            heads = []
            for hd in range(N_HEADS):
                ah = a[:, hd * HEAD_DIM:(hd + 1) * HEAD_DIM]
                heads.append(ah * cos2 + pltpu.roll(ah, HEAD_DIM // 2, 1) * sin2)
            return jnp.concatenate(heads, axis=1)

        q, k = rope(q), rope(k)

    p_ref[:, P_Q:P_Q + RET_WIDTH] = q.astype(BF16)
    p_ref[:, P_K:P_K + RET_WIDTH] = k.astype(BF16)
    v16 = v.astype(BF16)
    p_ref[:, P_V:P_V + RET_WIDTH] = v16
    p_ref[:, P_G:P_BG + CONV_WIDTH] = p[:, 3 * RET_WIDTH:4 * RET_WIDTH + CONV_WIDTH].astype(BF16)
    cg = p[:, 4 * RET_WIDTH + CONV_WIDTH:4 * RET_WIDTH + 2 * CONV_WIDTH]
    xc = p[:, 4 * RET_WIDTH + 2 * CONV_WIDTH:AB_IN]
    p_ref[:, P_CX:P_COLS] = (cg * xc).astype(BF16)

    tn = (((0,), (0,)), ((), ()))
    for c in reversed(range(n_chunks)):
        rows = slice(c * CHUNK, (c + 1) * CHUNK)
        kc = k[rows]
        kf = (kc * kdf_ref[...]).astype(BF16)
        kb = (kc * kdb_ref[...]).astype(BF16)
        vc = v16[rows]
        for hd in range(N_HEADS):
            cols = slice(hd * HEAD_DIM, (hd + 1) * HEAD_DIM)
            dsf_ref[c, hd] = lax.dot_general(kf[:, cols], vc[:, cols], tn, preferred_element_type=F32)
            dsb = lax.dot_general(kb[:, cols], vc[:, cols], tn, preferred_element_type=F32)
            sb = sb_ref[hd]
            sbp_ref[c, hd] = sb.astype(BF16)
            sb_ref[hd] = sb * cdb_ref[hd] + dsb


def _inproj(x2d, mod4, gain, w_in, tables, s0_b, *, n_seq, seq_len, tm, use_rope, mod_row):
    nt = seq_len // tm
    n_chunks = tm // CHUNK
    has_init = s0_b is not None
    tile = lambda b, j: b * nt + (nt - 1 - j)

    in_specs = [
        pl.BlockSpec((tm, D_MODEL), lambda b, j: (tile(b, j), 0)),
        pl.BlockSpec((None, None, 6, D_MODEL), lambda b, j: (0, mod_row(b), 0, 0)),
        pl.BlockSpec((1, D_MODEL), lambda b, j: (0, 0)),
        _resident((D_MODEL, AB_IN)),
    ]
    args = [x2d, mod4, gain, w_in]
    if use_rope:
        in_specs += [pl.BlockSpec((tm, HEAD_DIM), lambda b, j: (nt - 1 - j, 0))] * 2
        args += [tables["cos2"], tables["sin2"]]
    in_specs += [
        pl.BlockSpec((CHUNK, RET_WIDTH), lambda b, j: (0, 0)),
        pl.BlockSpec((CHUNK, RET_WIDTH), lambda b, j: (0, 0)),
        pl.BlockSpec((N_HEADS, 1, HEAD_DIM), lambda b, j: (0, 0, 0)),
    ]
    args += [tables["kdec_f"], tables["kdec_b"], tables["cd_b"]]
    state_spec = pl.BlockSpec((None, N_HEADS, HEAD_DIM, HEAD_DIM), lambda b, j: (b, 0, 0, 0))
    if has_init:
        in_specs.append(state_spec)
        args.append(s0_b)

    n_tok = n_seq * seq_len
    chunk_spec = pl.BlockSpec((n_chunks, N_HEADS, HEAD_DIM, HEAD_DIM), lambda b, j: (tile(b, j), 0, 0, 0))
    chunk_shape = (n_tok // CHUNK, N_HEADS, HEAD_DIM, HEAD_DIM)
    return pl.pallas_call(
        functools.partial(_inproj_kernel, n_chunks=n_chunks, use_rope=use_rope, has_init=has_init),
        grid=(n_seq, nt),
        in_specs=in_specs,
        out_specs=[
            pl.BlockSpec((tm, P_COLS), lambda b, j: (tile(b, j), 0)),
            chunk_spec,
            chunk_spec,
            state_spec,
        ],
        out_shape=[
            jax.ShapeDtypeStruct((n_tok, P_COLS), BF16),
            jax.ShapeDtypeStruct(chunk_shape, F32),
            jax.ShapeDtypeStruct(chunk_shape, BF16),
            jax.ShapeDtypeStruct((n_seq, N_HEADS, HEAD_DIM, HEAD_DIM), F32),
        ],
        compiler_params=pltpu.CompilerParams(
            dimension_semantics=("arbitrary", "arbitrary"), vmem_limit_bytes=VMEM_LIMIT_BYTES),
        name="inproj_bwd_scan",
    )(*args)


def _mix0_kernel(*refs, n_chunks, n_tiles, has_init):
    it = iter(refs)
    x_ref, p_ref, hprev_ref, hnext_ref, dsf_ref, sbp_ref = (next(it) for _ in range(6))
    s0_ref = next(it) if has_init else None
    (mod_ref, mask_ref, crf_ref, crb_ref, cdf_ref, retg_ref, convw_ref, convb_ref, wout_ref,
     gain_ref, wg_ref, wu_ref, wd_ref) = (next(it) for _ in range(13))
    xo_ref, sf_ref = next(it), next(it)
    y_scr = next(it)

    j = pl.program_id(1)
    tm = x_ref.shape[0]

    @pl.when(j == 0)
    def _():
        if has_init:
            sf_ref[...] = s0_ref[...]
        else:
            sf_ref[...] = jnp.zeros_like(sf_ref)

    nt_dims = (((1,), (1,)), ((), ()))
    for c in range(n_chunks):
        rows = slice(c * CHUNK, (c + 1) * CHUNK)
        for hd in range(N_HEADS):
            cols = slice(hd * HEAD_DIM, (hd + 1) * HEAD_DIM)
            qh = p_ref[rows, P_Q + hd * HEAD_DIM:P_Q + (hd + 1) * HEAD_DIM]
            kh = p_ref[rows, P_K + hd * HEAD_DIM:P_K + (hd + 1) * HEAD_DIM]
            vh = p_ref[rows, P_V + hd * HEAD_DIM:P_V + (hd + 1) * HEAD_DIM]
            scores = lax.dot_general(qh, kh, nt_dims, preferred_element_type=F32) * mask_ref[hd]
            sf = sf_ref[hd]
            o = (_dot(scores.astype(BF16), vh)
                 + _dot(qh, sf.astype(BF16)) * crf_ref[:, cols]
                 + _dot(qh, sbp_ref[c, hd]) * crb_ref[:, cols])
            sf_ref[hd] = sf * cdf_ref[hd] + dsf_ref[c, hd]
            o = o * lax.rsqrt(jnp.mean(o * o, axis=-1, keepdims=True) + EPS) * retg_ref[:, cols]
            gate = p_ref[rows, P_G + hd * HEAD_DIM:P_G + (hd + 1) * HEAD_DIM].astype(F32)
            y_scr[rows, cols] = (_silu(gate) * o).astype(BF16)

    cx = p_ref[:, P_CX:P_COLS].astype(F32)
    row_id = lax.broadcasted_iota(jnp.int32, cx.shape, 0)
    prev_row = hprev_ref[...].astype(F32)[HALO_ROWS - 1:HALO_ROWS] * jnp.where(j > 0, 1.0, 0.0)
    next_row = hnext_ref[...].astype(F32)[0:1] * jnp.where(j < n_tiles - 1, 1.0, 0.0)
    prev = jnp.where(row_id == 0, prev_row, pltpu.roll(cx, 1, 0))
    nxt = jnp.where(row_id == tm - 1, next_row, pltpu.roll(cx, tm - 1, 0))
    conv = convw_ref[0:1] * prev + convw_ref[1:2] * cx + convw_ref[2:3] * nxt + convb_ref[...]
    y_scr[:, RET_WIDTH:] = (p_ref[:, P_BG:P_BG + CONV_WIDTH].astype(F32) * conv).astype(BF16)

    mod = mod_ref[...]
    x1 = x_ref[...] + mod[2:3] * _dot(y_scr[...], wout_ref[...])
    xo_ref[...] = _ffn_residual(x1, mod, gain_ref[...], wg_ref, wu_ref, wd_ref)


def _mix0(x2d, p, dsf, sbp, mod4, tables, s0_f, w, *, n_seq, seq_len, tm, mod_row):
    nt = seq_len // tm
    n_chunks = tm // CHUNK
    has_init = s0_f is not None
    n_tok = n_seq * seq_len
    halo_per_tile = tm // HALO_ROWS
    last_halo = n_tok // HALO_ROWS - 1
    tile = lambda b, j: b * nt + j
    cx_block = P_CX // CONV_WIDTH

    chunk_spec = pl.BlockSpec((n_chunks, N_HEADS, HEAD_DIM, HEAD_DIM), lambda b, j: (tile(b, j), 0, 0, 0))
    state_spec = pl.BlockSpec((None, N_HEADS, HEAD_DIM, HEAD_DIM), lambda b, j: (b, 0, 0, 0))
    in_specs = [
        pl.BlockSpec((tm, D_MODEL), lambda b, j: (tile(b, j), 0)),
        pl.BlockSpec((tm, P_COLS), lambda b, j: (tile(b, j), 0)),
        pl.BlockSpec((HALO_ROWS, CONV_WIDTH),
                     lambda b, j: (jnp.maximum(tile(b, j) * halo_per_tile - 1, 0), cx_block)),
        pl.BlockSpec((HALO_ROWS, CONV_WIDTH),
                     lambda b, j: (jnp.minimum((tile(b, j) + 1) * halo_per_tile, last_halo), cx_block)),
        chunk_spec,
        chunk_spec,
    ]
    args = [x2d, p, p, p, dsf, sbp]
    if has_init:
        in_specs.append(state_spec)
        args.append(s0_f)
    in_specs += [
        pl.BlockSpec((None, None, 6, D_MODEL), lambda b, j: (0, mod_row(b), 0, 0)),
        _resident((N_HEADS, CHUNK, CHUNK)),
        _resident((CHUNK, RET_WIDTH)),
        _resident((CHUNK, RET_WIDTH)),
        _resident((N_HEADS, 1, HEAD_DIM)),
        _resident((1, RET_WIDTH)),
        _resident((3, CONV_WIDTH)),
        _resident((1, CONV_WIDTH)),
        _resident((D_MODEL, D_MODEL)),
        _resident((1, D_MODEL)),
        _resident((D_MODEL, FFN_HIDDEN)),
        _resident((D_MODEL, FFN_HIDDEN)),
        _resident((FFN_HIDDEN, D_MODEL)),
    ]
    args += [mod4, tables["mask"], tables["cross_f"], tables["cross_b"], tables["cd_f"], w["ret_g"],
             w["conv_w"], w["conv_b"], w["w_out_ab"], w["gain_ffn0"], w["w_gate0"], w["w_up0"], w["w_down0"]]
    return pl.pallas_call(
        functools.partial(_mix0_kernel, n_chunks=n_chunks, n_tiles=nt, has_init=has_init),
        grid=(n_seq, nt),
        in_specs=in_specs,
        out_specs=[pl.BlockSpec((tm, D_MODEL), lambda b, j: (tile(b, j), 0)), state_spec],
        out_shape=[
            jax.ShapeDtypeStruct((n_tok, D_MODEL), F32),
            jax.ShapeDtypeStruct((n_seq, N_HEADS, HEAD_DIM, HEAD_DIM), F32),
        ],
        scratch_shapes=[pltpu.VMEM((tm, D_MODEL), BF16)],
        compiler_params=pltpu.CompilerParams(
            dimension_semantics=("arbitrary", "arbitrary"), vmem_limit_bytes=VMEM_LIMIT_BYTES),
        name="retention_conv_ffn0",
    )(*args)


def _mix1_kernel(x_ref, mod_ref, gain_ref, win_ref, vg_ref, ws_ref, bs_ref, wout_ref, gain2_ref,
                 wg_ref, wu_ref, wd_ref, gfin_ref, o_ref, gated_scr, *, n_chunks):
    mod = mod_ref[...]
    x = x_ref[...]
    h = _norm_mod(x, gain_ref[...], mod[0:1], mod[1:2]).astype(BF16)
    z = _gelu_tanh(_dot(h, win_ref[...]))
    u = z[:, :D_MODEL]
    v = z[:, D_MODEL:]
    v = (v * lax.rsqrt(jnp.mean(v * v, axis=-1, keepdims=True) + EPS) * vg_ref[...]).astype(BF16)
    for c in range(n_chunks):
        rows = slice(c * CHUNK, (c + 1) * CHUNK)
        for g in range(CMLP_GROUPS):
            cols = slice(g * GROUP_WIDTH, (g + 1) * GROUP_WIDTH)
            s = _dot(ws_ref[g], v[rows, cols]) + bs_ref[g]
            gated_scr[rows, cols] = (u[rows, cols] * s).astype(BF16)
    x1 = x + mod[2:3] * _dot(gated_scr[...], wout_ref[...])
    x2 = _ffn_residual(x1, mod, gain2_ref[...], wg_ref, wu_ref, wd_ref)
    o_ref[...] = x2 * lax.rsqrt(jnp.mean(x2 * x2, axis=-1, keepdims=True) + EPS) * gfin_ref[...]


def _mix1(x2d, mod4, w, *, tm, mod_row):
    n_tok = x2d.shape[0]
    n_chunks = tm // CHUNK
    in_specs = [
        pl.BlockSpec((tm, D_MODEL), lambda i: (i, 0)),
        pl.BlockSpec((None, None, 6, D_MODEL), lambda i: (1, mod_row(i), 0, 0)),
        _resident((1, D_MODEL)),
        _resident((D_MODEL, 2 * D_MODEL)),
        _resident((1, D_MODEL)),
        _resident((CMLP_GROUPS, CHUNK, CHUNK)),
        _resident((CMLP_GROUPS, CHUNK, 1)),
        _resident((D_MODEL, D_MODEL)),
        _resident((1, D_MODEL)),
        _resident((D_MODEL, FFN_HIDDEN)),
        _resident((D_MODEL, FFN_HIDDEN)),
        _resident((FFN_HIDDEN, D_MODEL)),
        _resident((1, D_MODEL)),
    ]
    args = [x2d, mod4, w["gain_mix1"], w["w_in_c"], w["c_norm_g"], w["w_spatial"], w["b_spatial"],
            w["w_out_c"], w["gain_ffn1"], w["w_gate1"], w["w_up1"], w["w_down1"], w["final_g"]]
    return pl.pallas_call(
        functools.partial(_mix1_kernel, n_chunks=n_chunks),
        grid=(n_tok // tm,),
        in_specs=in_specs,
        out_specs=pl.BlockSpec((tm, D_MODEL), lambda i: (i, 0)),
        out_shape=jax.ShapeDtypeStruct((n_tok, D_MODEL), F32),
        scratch_shapes=[pltpu.VMEM((tm, D_MODEL), BF16)],
        compiler_params=pltpu.CompilerParams(
            dimension_semantics=("arbitrary",), vmem_limit_bytes=VMEM_LIMIT_BYTES),
        name="chunkmlp_ffn1",
    )(*args)


def _decay_tables(decay_logit):
    lg = jax.nn.log_sigmoid(decay_logit.astype(F32))
    idx = jnp.arange(CHUNK, dtype=F32)
    diff = idx[:, None] - idx[None, :]
    lg_f, lg_b = lg[0], lg[1]
    mask_f = jnp.where(diff >= 0, jnp.exp(jnp.maximum(diff, 0.0) * lg_f[:, None, None]), 0.0)
    mask_b = jnp.where(diff <= 0, jnp.exp(jnp.maximum(-diff, 0.0) * lg_b[:, None, None]), 0.0)
    per_head = lambda t: jnp.repeat(t.T, HEAD_DIM, axis=1)
    lanes = lambda t: jnp.broadcast_to(t[:, None, None], (N_HEADS, 1, HEAD_DIM))
    return {
        "mask": mask_f + mask_b,
        "cross_f": per_head(jnp.exp((idx + 1.0) * lg_f[:, None])),
        "cross_b": per_head(jnp.exp((CHUNK - idx) * lg_b[:, None])),
        "kdec_f": per_head(jnp.exp((CHUNK - 1.0 - idx) * lg_f[:, None])),
        "kdec_b": per_head(jnp.exp(idx * lg_b[:, None])),
        "cd_f": lanes(jnp.exp(CHUNK * lg_f)),
        "cd_b": lanes(jnp.exp(CHUNK * lg_b)),
    }


def _rope_tables(seq_len):
    rows = seq_len // GRID_W
    row = jnp.repeat(jnp.arange(rows, dtype=F32), GRID_W)
    col = jnp.tile(jnp.arange(GRID_W, dtype=F32), rows)
    nf = HEAD_DIM // 4
    freqs = ROPE_BASE ** (-jnp.arange(nf, dtype=F32) / nf)
    ang = jnp.concatenate([row[:, None] * freqs, col[:, None] * freqs], axis=-1)
    cos, sin = jnp.cos(ang), jnp.sin(ang)
    return {"cos2": jnp.concatenate([cos, cos], axis=-1), "sin2": jnp.concatenate([-sin, sin], axis=-1)}


def _trunk_pass(x, mod4, s0, w, tables, *, use_rope, mod_row_of_seq):
    n_seq, seq_len, _ = x.shape
    tm = min(TOKEN_TILE, seq_len)
    tiles_per_seq = seq_len // tm
    x2d = x.reshape(n_seq * seq_len, D_MODEL)
    s0_f = s0_b = None
    if s0 is not None:
        s0_f, s0_b = s0[:, 0], s0[:, 1]
    p, dsf, sbp, sb_fin = _inproj(x2d, mod4, w["gain_mix0"], w["w_in_ab"], tables, s0_b, n_seq=n_seq,
                                  seq_len=seq_len, tm=tm, use_rope=use_rope, mod_row=mod_row_of_seq)
    x1, sf_fin = _mix0(x2d, p, dsf, sbp, mod4, tables, s0_f, w, n_seq=n_seq, seq_len=seq_len, tm=tm,
                       mod_row=mod_row_of_seq)
    y = _mix1(x1, mod4, w, tm=tm, mod_row=lambda i: mod_row_of_seq(i // tiles_per_seq))
    return y.reshape(n_seq, seq_len, D_MODEL), sf_fin, sb_fin


def kernel(x_prompt, x_sample, state_ret, c, c_ctx, ada_w, ada_b, norm_mix_g, norm_ffn_g, w_in_ab,
           ret_decay_logit, ret_norm_g, conv_w, conv_b, w_out_ab, w_in_c, c_norm_g, w_spatial, b_spatial,
           w_out_c, w_gate, w_up, w_down, final_norm_g):
    n_lat = c.shape[0]
    cvecs = jnp.concatenate(
        [c_ctx[None, :], c, jnp.zeros((MOD_ROWS - 1 - n_lat, D_MODEL), F32)], axis=0)
    mod4 = _modulation(cvecs, ada_w, ada_b)

    row = lambda a: a.reshape(1, -1)
    w = {
        "gain_mix0": row(norm_mix_g[0]), "gain_mix1": row(norm_mix_g[1]),
        "gain_ffn0": row(norm_ffn_g[0]), "gain_ffn1": row(norm_ffn_g[1]),
        "w_in_ab": w_in_ab[0].astype(BF16), "ret_g": row(ret_norm_g[0]),
        "conv_w": conv_w[0], "conv_b": row(conv_b[0]), "w_out_ab": w_out_ab[0].astype(BF16),
        "w_in_c": w_in_c[0].astype(BF16), "c_norm_g": row(c_norm_g[0]),
        "w_spatial": w_spatial[0].astype(BF16), "b_spatial": b_spatial[0][:, :, None],
        "w_out_c": w_out_c[0].astype(BF16),
        "w_gate0": w_gate[0].astype(BF16), "w_up0": w_up[0].astype(BF16), "w_down0": w_down[0].astype(BF16),
        "w_gate1": w_gate[1].astype(BF16), "w_up1": w_up[1].astype(BF16), "w_down1": w_down[1].astype(BF16),
        "final_g": row(final_norm_g),
    }
    tables = _decay_tables(ret_decay_logit[0])

    y_prompt, sf_ctx, sb_ctx = _trunk_pass(x_prompt, mod4, None, w, tables, use_rope=False,
                                           mod_row_of_seq=lambda b: 0)
    lat_tables = dict(tables, **_rope_tables(x_sample.shape[1]))
    y_sample, _, _ = _trunk_pass(x_sample, mod4, state_ret[:, 0], w, lat_tables, use_rope=True,
                                 mod_row_of_seq=lambda b: b + 1)
    new_state = jnp.stack([sf_ctx, sb_ctx], axis=1)[:, None]
    return (y_prompt, y_sample, new_state)
```

```python
import functools
import math

import jax
import jax.numpy as jnp
from jax import lax
from jax.experimental import pallas as pl
from jax.experimental.pallas import tpu as pltpu

D_MODEL = 1024
N_HEADS = 4
HEAD_DIM = 128
RET_WIDTH = N_HEADS * HEAD_DIM
CONV_WIDTH = 512
CHUNK = 128
CMLP_GROUPS = 4
GROUP_WIDTH = D_MODEL // CMLP_GROUPS
FFN_HIDDEN = 2816
AB_IN = 4 * RET_WIDTH + 3 * CONV_WIDTH
GRID_W = 64
ROPE_BASE = 10000.0
EPS = 1e-6
MOD_ROWS = 16
MOD_COLS_PER_STEP = 1536

P_Q, P_K, P_V, P_G, P_BG, P_CX = 0, 512, 1024, 1536, 2048, 2560
P_COLS = 3072
HALO_ROWS = 16

TOKEN_TILE = 512
VMEM_LIMIT_BYTES = 56 * 1024 * 1024

F32 = jnp.float32
BF16 = jnp.bfloat16


def _dot(a, b):
    return jnp.dot(a, b, preferred_element_type=F32)


def _resident(shape):
    zeros = (0,) * len(shape)
    return pl.BlockSpec(shape, lambda *_: zeros, pipeline_mode=pl.Buffered(1))


def _norm_mod(x, gain, shift, scale):
    y = x * lax.rsqrt(jnp.mean(x * x, axis=-1, keepdims=True) + EPS)
    return (y * gain) * (1.0 + scale) + shift


def _silu(x):
    return x * jax.nn.sigmoid(x)


def _gelu_tanh(x):
    return 0.5 * x * (1.0 + jnp.tanh(math.sqrt(2.0 / math.pi) * (x + 0.044715 * (x * x * x))))


def _ffn_residual(x, mod, gain, wg_ref, wu_ref, wd_ref):
    h = _norm_mod(x, gain, mod[3:4], mod[4:5]).astype(BF16)
    act = (_silu(_dot(h, wg_ref[...])) * _dot(h, wu_ref[...])).astype(BF16)
    return x + mod[5:6] * _dot(act, wd_ref[...])


def _mod_kernel(cv_ref, w_ref, b_ref, o_ref):
    sc = _silu(cv_ref[...]).astype(BF16)
    o_ref[...] = _dot(sc, w_ref[...].astype(BF16)) + b_ref[...]


def _modulation(cvecs, ada_w, ada_b):
    depth, _, n_mod = ada_w.shape
    steps = n_mod // MOD_COLS_PER_STEP
    out = pl.pallas_call(
        _mod_kernel,
        grid=(depth, steps),
        in_specs=[
            pl.BlockSpec((MOD_ROWS, D_MODEL), lambda l, n: (0, 0)),
            pl.BlockSpec((None, D_MODEL, MOD_COLS_PER_STEP), lambda l, n: (l, 0, n)),
            pl.BlockSpec((None, 1, MOD_COLS_PER_STEP), lambda l, n: (l, 0, n)),
        ],
        out_specs=pl.BlockSpec((None, MOD_ROWS, MOD_COLS_PER_STEP), lambda l, n: (l, 0, n)),
        out_shape=jax.ShapeDtypeStruct((depth, MOD_ROWS, n_mod), F32),
        compiler_params=pltpu.CompilerParams(
            dimension_semantics=("arbitrary", "arbitrary"), vmem_limit_bytes=VMEM_LIMIT_BYTES),
        name="adaln_modulation",
    )(cvecs, ada_w, ada_b.reshape(depth, 1, n_mod))
    return out.reshape(depth, MOD_ROWS, 6, D_MODEL)


def _inproj_kernel(*refs, n_chunks, use_rope, has_init):
    it = iter(refs)
    x_ref, mod_ref, gain_ref, w_ref = next(it), next(it), next(it), next(it)
    cos_ref = sin_ref = s0_ref = None
    if use_rope:
        cos_ref, sin_ref = next(it), next(it)
    kdf_ref, kdb_ref, cdb_ref = next(it), next(it), next(it)
    if has_init:
        s0_ref = next(it)
    p_ref, dsf_ref, sbp_ref, sb_ref = next(it), next(it), next(it), next(it)

    @pl.when(pl.program_id(1) == 0)
    def _():
        if has_init:
            sb_ref[...] = s0_ref[...]
        else:
            sb_ref[...] = jnp.zeros_like(sb_ref)

    mod = mod_ref[...]
    h = _norm_mod(x_ref[...], gain_ref[...], mod[0:1], mod[1:2]).astype(BF16)
    p = _dot(h, w_ref[...])

    q = p[:, 0:RET_WIDTH] * (HEAD_DIM ** -0.5)
    k = p[:, RET_WIDTH:2 * RET_WIDTH]
    v = p[:, 2 * RET_WIDTH:3 * RET_WIDTH]
    if use_rope:
        cos2, sin2 = cos_ref[...], sin_ref[...]

        def rope(a):
            heads = []
            for hd in range(N_HEADS):
                ah = a[:, hd * HEAD_DIM:(hd + 1) * HEAD_DIM]
                heads.append(ah * cos2 + pltpu.roll(ah, HEAD_DIM // 2, 1) * sin2)
            return jnp.concatenate(heads, axis=1)

        q, k = rope(q), rope(k)

    p_ref[:, P_Q:P_Q + RET_WIDTH] = q.astype(BF16)
    p_ref[:, P_K:P_K + RET_WIDTH] = k.astype(BF16)
    v16 = v.astype(BF16)
    p_ref[:, P_V:P_V + RET_WIDTH] = v16
    p_ref[:, P_G:P_BG + CONV_WIDTH] = p[:, 3 * RET_WIDTH:4 * RET_WIDTH + CONV_WIDTH].astype(BF16)
    cg = p[:, 4 * RET_WIDTH + CONV_WIDTH:4 * RET_WIDTH + 2 * CONV_WIDTH]
    xc = p[:, 4 * RET_WIDTH + 2 * CONV_WIDTH:AB_IN]
    p_ref[:, P_CX:P_COLS] = (cg * xc).astype(BF16)

    tn = (((0,), (0,)), ((), ()))
    for c in reversed(range(n_chunks)):
        rows = slice(c * CHUNK, (c + 1) * CHUNK)
        kc = k[rows]
        kf = (kc * kdf_ref[...]).astype(BF16)
        kb = (kc * kdb_ref[...]).astype(BF16)
        vc = v16[rows]
        for hd in range(N_HEADS):
            cols = slice(hd * HEAD_DIM, (hd + 1) * HEAD_DIM)
            dsf_ref[c, hd] = lax.dot_general(kf[:, cols], vc[:, cols], tn, preferred_element_type=F32)
            dsb = lax.dot_general(kb[:, cols], vc[:, cols], tn, preferred_element_type=F32)
            sb = sb_ref[hd]
            sbp_ref[c, hd] = sb.astype(BF16)
            sb_ref[hd] = sb * cdb_ref[hd] + dsb


def _inproj(x2d, mod4, gain, w_in, tables, s0_b, *, n_seq, seq_len, tm, use_rope, mod_row):
    nt = seq_len // tm
    n_chunks = tm // CHUNK
    has_init = s0_b is not None
    tile = lambda b, j: b * nt + (nt - 1 - j)

    in_specs = [
        pl.BlockSpec((tm, D_MODEL), lambda b, j: (tile(b, j), 0)),
        pl.BlockSpec((None, None, 6, D_MODEL), lambda b, j: (0, mod_row(b), 0, 0)),
        pl.BlockSpec((1, D_MODEL), lambda b, j: (0, 0)),
        _resident((D_MODEL, AB_IN)),
    ]
    args = [x2d, mod4, gain, w_in]
    if use_rope:
        in_specs += [pl.BlockSpec((tm, HEAD_DIM), lambda b, j: (nt - 1 - j, 0))] * 2
        args += [tables["cos2"], tables["sin2"]]
    in_specs += [
        pl.BlockSpec((CHUNK, RET_WIDTH), lambda b, j: (0, 0)),
        pl.BlockSpec((CHUNK, RET_WIDTH), lambda b, j: (0, 0)),
        pl.BlockSpec((N_HEADS, 1, HEAD_DIM), lambda b, j: (0, 0, 0)),
    ]
    args += [tables["kdec_f"], tables["kdec_b"], tables["cd_b"]]
    state_spec = pl.BlockSpec((None, N_HEADS, HEAD_DIM, HEAD_DIM), lambda b, j: (b, 0, 0, 0))
    if has_init:
        in_specs.append(state_spec)
        args.append(s0_b)

    n_tok = n_seq * seq_len
    chunk_spec = pl.BlockSpec((n_chunks, N_HEADS, HEAD_DIM, HEAD_DIM), lambda b, j: (tile(b, j), 0, 0, 0))
    chunk_shape = (n_tok // CHUNK, N_HEADS, HEAD_DIM, HEAD_DIM)
    return pl.pallas_call(
        functools.partial(_inproj_kernel, n_chunks=n_chunks, use_rope=use_rope, has_init=has_init),
        grid=(n_seq, nt),
        in_specs=in_specs,
        out_specs=[
            pl.BlockSpec((tm, P_COLS), lambda b, j: (tile(b, j), 0)),
            chunk_spec,
            chunk_spec,
            state_spec,
        ],
        out_shape=[
            jax.ShapeDtypeStruct((n_tok, P_COLS), BF16),
            jax.ShapeDtypeStruct(chunk_shape, F32),
            jax.ShapeDtypeStruct(chunk_shape, BF16),
            jax.ShapeDtypeStruct((n_seq, N_HEADS, HEAD_DIM, HEAD_DIM), F32),
        ],
        compiler_params=pltpu.CompilerParams(
            dimension_semantics=("arbitrary", "arbitrary"), vmem_limit_bytes=VMEM_LIMIT_BYTES),
        name="inproj_bwd_scan",
    )(*args)


def _mix0_kernel(*refs, n_chunks, n_tiles, has_init):
    it = iter(refs)
    x_ref, p_ref, hprev_ref, hnext_ref, dsf_ref, sbp_ref = (next(it) for _ in range(6))
    s0_ref = next(it) if has_init else None
    (mod_ref, mask_ref, crf_ref, crb_ref, cdf_ref, retg_ref, convw_ref, convb_ref, wout_ref,
     gain_ref, wg_ref, wu_ref, wd_ref) = (next(it) for _ in range(13))
    xo_ref, sf_ref = next(it), next(it)
    y_scr = next(it)

    j = pl.program_id(1)
    tm = x_ref.shape[0]

    @pl.when(j == 0)
    def _():
        if has_init:
            sf_ref[...] = s0_ref[...]
        else:
            sf_ref[...] = jnp.zeros_like(sf_ref)

    nt_dims = (((1,), (1,)), ((), ()))
    for c in range(n_chunks):
        rows = slice(c * CHUNK, (c + 1) * CHUNK)
        for hd in range(N_HEADS):
            cols = slice(hd * HEAD_DIM, (hd + 1) * HEAD_DIM)
            qh = p_ref[rows, P_Q + hd * HEAD_DIM:P_Q + (hd + 1) * HEAD_DIM]
            kh = p_ref[rows, P_K + hd * HEAD_DIM:P_K + (hd + 1) * HEAD_DIM]
            vh = p_ref[rows, P_V + hd * HEAD_DIM:P_V + (hd + 1) * HEAD_DIM]
            scores = lax.dot_general(qh, kh, nt_dims, preferred_element_type=F32) * mask_ref[hd]
            sf = sf_ref[hd]
            o = (_dot(scores.astype(BF16), vh)
                 + _dot(qh, sf.astype(BF16)) * crf_ref[:, cols]
                 + _dot(qh, sbp_ref[c, hd]) * crb_ref[:, cols])
            sf_ref[hd] = sf * cdf_ref[hd] + dsf_ref[c, hd]
            o = o * lax.rsqrt(jnp.mean(o * o, axis=-1, keepdims=True) + EPS) * retg_ref[:, cols]
            gate = p_ref[rows, P_G + hd * HEAD_DIM:P_G + (hd + 1) * HEAD_DIM].astype(F32)
            y_scr[rows, cols] = (_silu(gate) * o).astype(BF16)

    cx = p_ref[:, P_CX:P_COLS].astype(F32)
    row_id = lax.broadcasted_iota(jnp.int32, cx.shape, 0)
    prev_row = hprev_ref[...].astype(F32)[HALO_ROWS - 1:HALO_ROWS] * jnp.where(j > 0, 1.0, 0.0)
    next_row = hnext_ref[...].astype(F32)[0:1] * jnp.where(j < n_tiles - 1, 1.0, 0.0)
    prev = jnp.where(row_id == 0, prev_row, pltpu.roll(cx, 1, 0))
    nxt = jnp.where(row_id == tm - 1, next_row, pltpu.roll(cx, tm - 1, 0))
    conv = convw_ref[0:1] * prev + convw_ref[1:2] * cx + convw_ref[2:3] * nxt + convb_ref[...]
    y_scr[:, RET_WIDTH:] = (p_ref[:, P_BG:P_BG + CONV_WIDTH].astype(F32) * conv).astype(BF16)

    mod = mod_ref[...]
    x1 = x_ref[...] + mod[2:3] * _dot(y_scr[...], wout_ref[...])
    xo_ref[...] = _ffn_residual(x1, mod, gain_ref[...], wg_ref, wu_ref, wd_ref)


def _mix0(x2d, p, dsf, sbp, mod4, tables, s0_f, w, *, n_seq, seq_len, tm, mod_row):
    nt = seq_len // tm
    n_chunks = tm // CHUNK
    has_init = s0_f is not None
    n_tok = n_seq * seq_len
    halo_per_tile = tm // HALO_ROWS
    last_halo = n_tok // HALO_ROWS - 1
    tile = lambda b, j: b * nt + j
    cx_block = P_CX // CONV_WIDTH

    chunk_spec = pl.BlockSpec((n_chunks, N_HEADS, HEAD_DIM, HEAD_DIM), lambda b, j: (tile(b, j), 0, 0, 0))
    state_spec = pl.BlockSpec((None, N_HEADS, HEAD_DIM, HEAD_DIM), lambda b, j: (b, 0, 0, 0))
    in_specs = [
        pl.BlockSpec((tm, D_MODEL), lambda b, j: (tile(b, j), 0)),
        pl.BlockSpec((tm, P_COLS), lambda b, j: (tile(b, j), 0)),
        pl.BlockSpec((HALO_ROWS, CONV_WIDTH),
                     lambda b, j: (jnp.maximum(tile(b, j) * halo_per_tile - 1, 0), cx_block)),
        pl.BlockSpec((HALO_ROWS, CONV_WIDTH),
                     lambda b, j: (jnp.minimum((tile(b, j) + 1) * halo_per_tile, last_halo), cx_block)),
        chunk_spec,
        chunk_spec,
    ]
    args = [x2d, p, p, p, dsf, sbp]
    if has_init:
        in_specs.append(state_spec)
        args.append(s0_f)
    in_specs += [
        pl.BlockSpec((None, None, 6, D_MODEL), lambda b, j: (0, mod_row(b), 0, 0)),
        _resident((N_HEADS, CHUNK, CHUNK)),
        _resident((CHUNK, RET_WIDTH)),
        _resident((CHUNK, RET_WIDTH)),
        _resident((N_HEADS, 1, HEAD_DIM)),
        _resident((1, RET_WIDTH)),
        _resident((3, CONV_WIDTH)),
        _resident((1, CONV_WIDTH)),
        _resident((D_MODEL, D_MODEL)),
        _resident((1, D_MODEL)),
        _resident((D_MODEL, FFN_HIDDEN)),
        _resident((D_MODEL, FFN_HIDDEN)),
        _resident((FFN_HIDDEN, D_MODEL)),
    ]
    args += [mod4, tables["mask"], tables["cross_f"], tables["cross_b"], tables["cd_f"], w["ret_g"],
             w["conv_w"], w["conv_b"], w["w_out_ab"], w["gain_ffn0"], w["w_gate0"], w["w_up0"], w["w_down0"]]
    return pl.pallas_call(
        functools.partial(_mix0_kernel, n_chunks=n_chunks, n_tiles=nt, has_init=has_init),
        grid=(n_seq, nt),
        in_specs=in_specs,
        out_specs=[pl.BlockSpec((tm, D_MODEL), lambda b, j: (tile(b, j), 0)), state_spec],
        out_shape=[
            jax.ShapeDtypeStruct((n_tok, D_MODEL), F32),
            jax.ShapeDtypeStruct((n_seq, N_HEADS, HEAD_DIM, HEAD_DIM), F32),
        ],
        scratch_shapes=[pltpu.VMEM((tm, D_MODEL), BF16)],
        compiler_params=pltpu.CompilerParams(
            dimension_semantics=("arbitrary", "arbitrary"), vmem_limit_bytes=VMEM_LIMIT_BYTES),
        name="retention_conv_ffn0",
    )(*args)


def _mix1_kernel(x_ref, mod_ref, gain_ref, win_ref, vg_ref, ws_ref, bs_ref, wout_ref, gain2_ref,
                 wg_ref, wu_ref, wd_ref, gfin_ref, o_ref, gated_scr, *, n_chunks):
    mod = mod_ref[...]
    x = x_ref[...]
    h = _norm_mod(x, gain_ref[...], mod[0:1], mod[1:2]).astype(BF16)
    z = _gelu_tanh(_dot(h, win_ref[...]))
    u = z[:, :D_MODEL]
    v = z[:, D_MODEL:]
    v = (v * lax.rsqrt(jnp.mean(v * v, axis=-1, keepdims=True) + EPS) * vg_ref[...]).astype(BF16)
    for c in range(n_chunks):
        rows = slice(c * CHUNK, (c + 1) * CHUNK)
        for g in range(CMLP_GROUPS):
            cols = slice(g * GROUP_WIDTH, (g + 1) * GROUP_WIDTH)
            s = _dot(ws_ref[g], v[rows, cols]) + bs_ref[g]
            gated_scr[rows, cols] = (u[rows, cols] * s).astype(BF16)
    x1 = x + mod[2:3] * _dot(gated_scr[...], wout_ref[...])
    x2 = _ffn_residual(x1, mod, gain2_ref[...], wg_ref, wu_ref, wd_ref)
    o_ref[...] = x2 * lax.rsqrt(jnp.mean(x2 * x2, axis=-1, keepdims=True) + EPS) * gfin_ref[...]


def _mix1(x2d, mod4, w, *, tm, mod_row):
    n_tok = x2d.shape[0]
    n_chunks = tm // CHUNK
    in_specs = [
        pl.BlockSpec((tm, D_MODEL), lambda i: (i, 0)),
        pl.BlockSpec((None, None, 6, D_MODEL), lambda i: (1, mod_row(i), 0, 0)),
        _resident((1, D_MODEL)),
        _resident((D_MODEL, 2 * D_MODEL)),
        _resident((1, D_MODEL)),
        _resident((CMLP_GROUPS, CHUNK, CHUNK)),
        _resident((CMLP_GROUPS, CHUNK, 1)),
        _resident((D_MODEL, D_MODEL)),
        _resident((1, D_MODEL)),
        _resident((D_MODEL, FFN_HIDDEN)),
        _resident((D_MODEL, FFN_HIDDEN)),
        _resident((FFN_HIDDEN, D_MODEL)),
        _resident((1, D_MODEL)),
    ]
    args = [x2d, mod4, w["gain_mix1"], w["w_in_c"], w["c_norm_g"], w["w_spatial"], w["b_spatial"],
            w["w_out_c"], w["gain_ffn1"], w["w_gate1"], w["w_up1"], w["w_down1"], w["final_g"]]
    return pl.pallas_call(
        functools.partial(_mix1_kernel, n_chunks=n_chunks),
        grid=(n_tok // tm,),
        in_specs=in_specs,
        out_specs=pl.BlockSpec((tm, D_MODEL), lambda i: (i, 0)),
        out_shape=jax.ShapeDtypeStruct((n_tok, D_MODEL), F32),
        scratch_shapes=[pltpu.VMEM((tm, D_MODEL), BF16)],
        compiler_params=pltpu.CompilerParams(
            dimension_semantics=("arbitrary",), vmem_limit_bytes=VMEM_LIMIT_BYTES),
        name="chunkmlp_ffn1",
    )(*args)


def _decay_tables(decay_logit):
    lg = jax.nn.log_sigmoid(decay_logit.astype(F32))
    idx = jnp.arange(CHUNK, dtype=F32)
    diff = idx[:, None] - idx[None, :]
    lg_f, lg_b = lg[0], lg[1]
    mask_f = jnp.where(diff >= 0, jnp.exp(jnp.maximum(diff, 0.0) * lg_f[:, None, None]), 0.0)
    mask_b = jnp.where(diff <= 0, jnp.exp(jnp.maximum(-diff, 0.0) * lg_b[:, None, None]), 0.0)
    per_head = lambda t: jnp.repeat(t.T, HEAD_DIM, axis=1)
    lanes = lambda t: jnp.broadcast_to(t[:, None, None], (N_HEADS, 1, HEAD_DIM))
    return {
        "mask": mask_f + mask_b,
        "cross_f": per_head(jnp.exp((idx + 1.0) * lg_f[:, None])),
        "cross_b": per_head(jnp.exp((CHUNK - idx) * lg_b[:, None])),
        "kdec_f": per_head(jnp.exp((CHUNK - 1.0 - idx) * lg_f[:, None])),
        "kdec_b": per_head(jnp.exp(idx * lg_b[:, None])),
        "cd_f": lanes(jnp.exp(CHUNK * lg_f)),
        "cd_b": lanes(jnp.exp(CHUNK * lg_b)),
    }


def _rope_tables(seq_len):
    rows = seq_len // GRID_W
    row = jnp.repeat(jnp.arange(rows, dtype=F32), GRID_W)
    col = jnp.tile(jnp.arange(GRID_W, dtype=F32), rows)
    nf = HEAD_DIM // 4
    freqs = ROPE_BASE ** (-jnp.arange(nf, dtype=F32) / nf)
    ang = jnp.concatenate([row[:, None] * freqs, col[:, None] * freqs], axis=-1)
    cos, sin = jnp.cos(ang), jnp.sin(ang)
    return {"cos2": jnp.concatenate([cos, cos], axis=-1), "sin2": jnp.concatenate([-sin, sin], axis=-1)}


def _trunk_pass(x, mod4, s0, w, tables, *, use_rope, mod_row_of_seq):
    n_seq, seq_len, _ = x.shape
    tm = min(TOKEN_TILE, seq_len)
    tiles_per_seq = seq_len // tm
    x2d = x.reshape(n_seq * seq_len, D_MODEL)
    s0_f = s0_b = None
    if s0 is not None:
        s0_f, s0_b = s0[:, 0], s0[:, 1]
    p, dsf, sbp, sb_fin = _inproj(x2d, mod4, w["gain_mix0"], w["w_in_ab"], tables, s0_b, n_seq=n_seq,
                                  seq_len=seq_len, tm=tm, use_rope=use_rope, mod_row=mod_row_of_seq)
    x1, sf_fin = _mix0(x2d, p, dsf, sbp, mod4, tables, s0_f, w, n_seq=n_seq, seq_len=seq_len, tm=tm,
                       mod_row=mod_row_of_seq)
    y = _mix1(x1, mod4, w, tm=tm, mod_row=lambda i: mod_row_of_seq(i // tiles_per_seq))
    return y.reshape(n_seq, seq_len, D_MODEL), sf_fin, sb_fin


def kernel(x_prompt, x_sample, state_ret, c, c_ctx, ada_w, ada_b, norm_mix_g, norm_ffn_g, w_in_ab,
           ret_decay_logit, ret_norm_g, conv_w, conv_b, w_out_ab, w_in_c, c_norm_g, w_spatial, b_spatial,
           w_out_c, w_gate, w_up, w_down, final_norm_g):
    n_lat = c.shape[0]
    cvecs = jnp.concatenate(
        [c_ctx[None, :], c, jnp.zeros((MOD_ROWS - 1 - n_lat, D_MODEL), F32)], axis=0)
    mod4 = _modulation(cvecs, ada_w, ada_b)

    row = lambda a: a.reshape(1, -1)
    w = {
        "gain_mix0": row(norm_mix_g[0]), "gain_mix1": row(norm_mix_g[1]),
        "gain_ffn0": row(norm_ffn_g[0]), "gain_ffn1": row(norm_ffn_g[1]),
        "w_in_ab": w_in_ab[0].astype(BF16), "ret_g": row(ret_norm_g[0]),
        "conv_w": conv_w[0], "conv_b": row(conv_b[0]), "w_out_ab": w_out_ab[0].astype(BF16),
        "w_in_c": w_in_c[0].astype(BF16), "c_norm_g": row(c_norm_g[0]),
        "w_spatial": w_spatial[0].astype(BF16), "b_spatial": b_spatial[0][:, :, None],
        "w_out_c": w_out_c[0].astype(BF16),
        "w_gate0": w_gate[0].astype(BF16), "w_up0": w_up[0].astype(BF16), "w_down0": w_down[0].astype(BF16),
        "w_gate1": w_gate[1].astype(BF16), "w_up1": w_up[1].astype(BF16), "w_down1": w_down[1].astype(BF16),
        "final_g": row(final_norm_g),
    }
    tables = _decay_tables(ret_decay_logit[0])

    y_prompt, sf_ctx, sb_ctx = _trunk_pass(x_prompt, mod4, None, w, tables, use_rope=False,
                                           mod_row_of_seq=lambda b: 0)
    lat_tables = dict(tables, **_rope_tables(x_sample.shape[1]))
    y_sample, _, _ = _trunk_pass(x_sample, mod4, state_ret[:, 0], w, lat_tables, use_rope=True,
                                 mod_row_of_seq=lambda b: b + 1)
    new_state = jnp.stack([sf_ctx, sb_ctx], axis=1)[:, None]
    return (y_prompt, y_sample, new_state)
```

```python
import functools
import math

import numpy as np
import jax
import jax.numpy as jnp
from jax import lax
from jax.experimental import pallas as pl
from jax.experimental.pallas import tpu as pltpu

D_MODEL = 1024
N_HEADS = 4
HEAD_DIM = 128
RET_WIDTH = N_HEADS * HEAD_DIM
CONV_WIDTH = 512
CHUNK = 128
CMLP_GROUPS = 4
GROUP_WIDTH = D_MODEL // CMLP_GROUPS
FFN_HIDDEN = 2816
AB_IN = 4 * RET_WIDTH + 3 * CONV_WIDTH
GRID_W = 64
ROPE_BASE = 10000.0
EPS = 1e-6
MOD_ROWS = 16
MOD_COLS_PER_STEP = 1536

P_Q, P_K, P_V, P_G, P_BG, P_CX = 0, 512, 1024, 1536, 2048, 2560
P_COLS = 3072
HALO_ROWS = 16

MXU_TILE = 256
TOKEN_TILE = 512
FFN_SLABS = tuple((lo, lo + MXU_TILE) for lo in range(0, FFN_HIDDEN, MXU_TILE))
VMEM_LIMIT_BYTES = 56 * 1024 * 1024

F32 = jnp.float32
BF16 = jnp.bfloat16


def _dot(a, b):
    return jnp.dot(a, b, preferred_element_type=F32)


def _resident(shape, layer=None):
    if layer is None:
        zeros = (0,) * len(shape)
        return pl.BlockSpec(shape, lambda *_: zeros, pipeline_mode=pl.Buffered(1))
    index = (layer,) + (0,) * len(shape)
    return pl.BlockSpec((None,) + tuple(shape), lambda *_: index, pipeline_mode=pl.Buffered(1))


def _norm_mod(x, gain, shift, scale):
    y = x * lax.rsqrt(jnp.mean(x * x, axis=-1, keepdims=True) + EPS)
    return (y * gain) * (1.0 + scale) + shift


def _silu(x):
    return x * jax.nn.sigmoid(x)


def _gelu_tanh(x):
    return 0.5 * x * (1.0 + jnp.tanh(math.sqrt(2.0 / math.pi) * (x + 0.044715 * (x * x * x))))


def _pipelined(first_halves, second_halves):
    order = [first_halves[0]]
    for k in range(1, len(first_halves)):
        order += [first_halves[k], second_halves[k - 1]]
    order.append(second_halves[-1])
    return order


def _weave(primary, filler, gaps):
    filler = list(filler)
    order = []
    for phase, gap in zip(primary, gaps):
        order += filler[:gap] + [phase]
        filler = filler[gap:]
    return order + filler


def _run(phases):
    for phase in phases:
        phase()


def _ffn_phases(h2_scr, acc_scr, wg_ref, wu_ref, wd_ref):
    acts = {}

    def up(k):
        def phase():
            lo, hi = FFN_SLABS[k]
            h2 = h2_scr[...]
            acts[k] = (_silu(_dot(h2, wg_ref[:, lo:hi])) * _dot(h2, wu_ref[:, lo:hi])).astype(BF16)
        return phase

    def down(k):
        def phase():
            lo, hi = FFN_SLABS[k]
            part = _dot(acts.pop(k), wd_ref[lo:hi, :])
            if k == 0:
                acc_scr[...] = part
            else:
                acc_scr[...] += part
        return phase

    slabs = range(len(FFN_SLABS))
    return _pipelined([up(k) for k in slabs], [down(k) for k in slabs])


def _mod_kernel(cv_ref, w_ref, b_ref, o_ref):
    sc = _silu(cv_ref[...]).astype(BF16)
    o_ref[...] = _dot(sc, w_ref[...].astype(BF16)) + b_ref[...]


def _modulation(cvecs, ada_w, ada_b):
    depth, _, n_mod = ada_w.shape
    steps = n_mod // MOD_COLS_PER_STEP
    out = pl.pallas_call(
        _mod_kernel,
        grid=(depth, steps),
        in_specs=[
            pl.BlockSpec((MOD_ROWS, D_MODEL), lambda l, n: (0, 0)),
            pl.BlockSpec((None, D_MODEL, MOD_COLS_PER_STEP), lambda l, n: (l, 0, n)),
            pl.BlockSpec((None, 1, MOD_COLS_PER_STEP), lambda l, n: (l, 0, n)),
        ],
        out_specs=pl.BlockSpec((None, MOD_ROWS, MOD_COLS_PER_STEP), lambda l, n: (l, 0, n)),
        out_shape=jax.ShapeDtypeStruct((depth, MOD_ROWS, n_mod), F32),
        compiler_params=pltpu.CompilerParams(
            dimension_semantics=("arbitrary", "arbitrary"), vmem_limit_bytes=VMEM_LIMIT_BYTES),
        name="adaln_modulation",
    )(cvecs, ada_w, ada_b.reshape(depth, 1, n_mod))
    return out.reshape(depth, MOD_ROWS, 6, D_MODEL)


def _inproj_kernel(*refs, n_chunks, use_rope, has_init):
    it = iter(refs)
    x_ref, mod_ref, gain_ref, w_ref = next(it), next(it), next(it), next(it)
    cos_ref = sin_ref = s0_ref = None
    if use_rope:
        cos_ref, sin_ref = next(it), next(it)
    kdf_ref, kdb_ref, cdb_ref = next(it), next(it), next(it)
    if has_init:
        s0_ref = next(it)
    p_ref, dsf_ref, sbp_ref, sb_ref = next(it), next(it), next(it), next(it)

    @pl.when(pl.program_id(1) == 0)
    def _():
        if has_init:
            sb_ref[...] = s0_ref[...]
        else:
            sb_ref[...] = jnp.zeros_like(sb_ref)

    mod = mod_ref[...]
    h = _norm_mod(x_ref[...], gain_ref[...], mod[0:1], mod[1:2]).astype(BF16)
    p = _dot(h, w_ref[...])

    q = p[:, 0:RET_WIDTH] * (HEAD_DIM ** -0.5)
    k = p[:, RET_WIDTH:2 * RET_WIDTH]
    v = p[:, 2 * RET_WIDTH:3 * RET_WIDTH]
    if use_rope:
        cos2, sin2 = cos_ref[...], sin_ref[...]

        def rope(a):
            heads = []
            for hd in range(N_HEADS):
                ah = a[:, hd * HEAD_DIM:(hd + 1) * HEAD_DIM]
                heads.append(ah * cos2 + pltpu.roll(ah, HEAD_DIM // 2, 1) * sin2)
            return jnp.concatenate(heads, axis=1)

        q, k = rope(q), rope(k)

    p_ref[:, P_Q:P_Q + RET_WIDTH] = q.astype(BF16)
    p_ref[:, P_K:P_K + RET_WIDTH] = k.astype(BF16)
    v16 = v.astype(BF16)
    p_ref[:, P_V:P_V + RET_WIDTH] = v16
    p_ref[:, P_G:P_BG + CONV_WIDTH] = p[:, 3 * RET_WIDTH:4 * RET_WIDTH + CONV_WIDTH].astype(BF16)
    cg = p[:, 4 * RET_WIDTH + CONV_WIDTH:4 * RET_WIDTH + 2 * CONV_WIDTH]
    xc = p[:, 4 * RET_WIDTH + 2 * CONV_WIDTH:AB_IN]
    p_ref[:, P_CX:P_COLS] = (cg * xc).astype(BF16)

    tn = (((0,), (0,)), ((), ()))
    for c in reversed(range(n_chunks)):
        rows = slice(c * CHUNK, (c + 1) * CHUNK)
        kc = k[rows]
        kf = (kc * kdf_ref[...]).astype(BF16)
        kb = (kc * kdb_ref[...]).astype(BF16)
        vc = v16[rows]
        for hd in range(N_HEADS):
            cols = slice(hd * HEAD_DIM, (hd + 1) * HEAD_DIM)
            dsf_ref[c, hd] = lax.dot_general(kf[:, cols], vc[:, cols], tn, preferred_element_type=F32)
            dsb = lax.dot_general(kb[:, cols], vc[:, cols], tn, preferred_element_type=F32)
            sb = sb_ref[hd]
            sbp_ref[c, hd] = sb.astype(BF16)
            sb_ref[hd] = sb * cdb_ref[hd] + dsb


def _state_spec(index_of_seq, direction):
    return pl.BlockSpec((None, None, None, N_HEADS, HEAD_DIM, HEAD_DIM),
                        lambda *g: (index_of_seq(*g), 0, direction, 0, 0, 0))


def _inproj(x2d, mod4, w, tables, state, *, n_seq, seq_len, tm, use_rope, mod_row):
    nt = seq_len // tm
    n_chunks = tm // CHUNK
    has_init = state is not None
    tile = lambda b, j: b * nt + (nt - 1 - j)

    in_specs = [
        pl.BlockSpec((tm, D_MODEL), lambda b, j: (tile(b, j), 0)),
        pl.BlockSpec((None, None, 6, D_MODEL), lambda b, j: (0, mod_row(tile(b, j)), 0, 0)),
        _resident((1, D_MODEL), layer=0),
        _resident((D_MODEL, AB_IN), layer=0),
    ]
    args = [x2d, mod4, w["norm_mix_g"], w["w_in_ab"]]
    if use_rope:
        in_specs += [pl.BlockSpec((tm, HEAD_DIM), lambda b, j: (nt - 1 - j, 0))] * 2
        args += [tables["cos2"], tables["sin2"]]
    in_specs += [
        pl.BlockSpec((CHUNK, RET_WIDTH), lambda b, j: (0, 0)),
        pl.BlockSpec((CHUNK, RET_WIDTH), lambda b, j: (0, 0)),
        pl.BlockSpec((N_HEADS, 1, HEAD_DIM), lambda b, j: (0, 0, 0)),
    ]
    args += [tables["kdec_f"], tables["kdec_b"], tables["cd_b"]]
    if has_init:
        in_specs.append(_state_spec(lambda b, j: b, 1))
        args.append(state)

    n_tok = n_seq * seq_len
    chunk_spec = pl.BlockSpec((n_chunks, N_HEADS, HEAD_DIM, HEAD_DIM), lambda b, j: (tile(b, j), 0, 0, 0))
    chunk_shape = (n_tok // CHUNK, N_HEADS, HEAD_DIM, HEAD_DIM)
    return pl.pallas_call(
        functools.partial(_inproj_kernel, n_chunks=n_chunks, use_rope=use_rope, has_init=has_init),
        grid=(n_seq, nt),
        in_specs=in_specs,
        out_specs=[
            pl.BlockSpec((tm, P_COLS), lambda b, j: (tile(b, j), 0)),
            chunk_spec,
            chunk_spec,
            pl.BlockSpec((None, N_HEADS, HEAD_DIM, HEAD_DIM), lambda b, j: (b, 0, 0, 0)),
        ],
        out_shape=[
            jax.ShapeDtypeStruct((n_tok, P_COLS), BF16),
            jax.ShapeDtypeStruct(chunk_shape, F32),
            jax.ShapeDtypeStruct(chunk_shape, BF16),
            jax.ShapeDtypeStruct((n_seq, N_HEADS, HEAD_DIM, HEAD_DIM), F32),
        ],
        compiler_params=pltpu.CompilerParams(
            dimension_semantics=("arbitrary", "arbitrary"), vmem_limit_bytes=VMEM_LIMIT_BYTES),
        name="inproj_bwd_scan",
    )(*args)


def _mix0_kernel(*refs, n_chunks, tiles_per_seq, n_tiles, has_init):
    it = iter(refs)
    x_ref, p_ref, hprev_ref, hnext_ref, dsf_ref, sbp_ref = (next(it) for _ in range(6))
    s0_ref = next(it) if has_init else None
    (mod_ref, modp_ref, mask_ref, crf_ref, crb_ref, cdf_ref, retg_ref, convw_ref, convb_ref, wout_ref,
     gain_ref, wg_ref, wu_ref, wd_ref) = (next(it) for _ in range(14))
    xo_ref, sf_ref = next(it), next(it)
    y_scr, x1_scr, h2_scr, acc_scr = (next(it) for _ in range(4))

    i = pl.program_id(0)
    j = lax.rem(i, tiles_per_seq)
    tm = x_ref.shape[0]

    @pl.when(jnp.logical_and(j == 0, i < n_tiles))
    def _():
        if has_init:
            sf_ref[...] = s0_ref[...]
        else:
            sf_ref[...] = jnp.zeros_like(sf_ref)

    def mixer_phases():
        nt_dims = (((1,), (1,)), ((), ()))
        live = {}

        def head_cols(base, hd):
            return slice(base + hd * HEAD_DIM, base + (hd + 1) * HEAD_DIM)

        def scores_and_cross(c):
            def phase():
                rows = slice(c * CHUNK, (c + 1) * CHUNK)
                for hd in range(N_HEADS):
                    cols = head_cols(0, hd)
                    qh = p_ref[rows, head_cols(P_Q, hd)]
                    kh = p_ref[rows, head_cols(P_K, hd)]
                    scores = lax.dot_general(qh, kh, nt_dims, preferred_element_type=F32) * mask_ref[hd]
                    sf = sf_ref[hd]
                    states = jnp.concatenate([sf.astype(BF16), sbp_ref[c, hd]], axis=1)
                    qs = _dot(qh, states)
                    cross = qs[:, :HEAD_DIM] * crf_ref[:, cols] + qs[:, HEAD_DIM:] * crb_ref[:, cols]
                    sf_ref[hd] = sf * cdf_ref[hd] + dsf_ref[c, hd]
                    live[c, hd] = (scores.astype(BF16), cross)
            return phase

        def retention_out(c):
            def phase():
                rows = slice(c * CHUNK, (c + 1) * CHUNK)
                for hd in range(N_HEADS):
                    cols = head_cols(0, hd)
                    scores, cross = live.pop((c, hd))
                    o = _dot(scores, p_ref[rows, head_cols(P_V, hd)]) + cross
                    o = o * lax.rsqrt(jnp.mean(o * o, axis=-1, keepdims=True) + EPS) * retg_ref[:, cols]
                    gate = p_ref[rows, head_cols(P_G, hd)].astype(F32)
                    y_scr[rows, cols] = (_silu(gate) * o).astype(BF16)
            return phase

        def conv():
            cx = p_ref[:, P_CX:P_COLS].astype(F32)
            row_id = lax.broadcasted_iota(jnp.int32, cx.shape, 0)
            prev_row = hprev_ref[...].astype(F32)[HALO_ROWS - 1:HALO_ROWS] * jnp.where(j > 0, 1.0, 0.0)
            next_row = hnext_ref[...].astype(F32)[0:1] * jnp.where(j < tiles_per_seq - 1, 1.0, 0.0)
            prev = jnp.where(row_id == 0, prev_row, pltpu.roll(cx, 1, 0))
            nxt = jnp.where(row_id == tm - 1, next_row, pltpu.roll(cx, tm - 1, 0))
            out = convw_ref[0:1] * prev + convw_ref[1:2] * cx + convw_ref[2:3] * nxt + convb_ref[...]
            y_scr[:, RET_WIDTH:] = (p_ref[:, P_BG:P_BG + CONV_WIDTH].astype(F32) * out).astype(BF16)

        def out_proj():
            live["x1"] = x_ref[...] + mod_ref[2:3] * _dot(y_scr[...], wout_ref[...])

        def norm_ffn():
            live["h2"] = _norm_mod(live["x1"], gain_ref[...], mod_ref[3:4], mod_ref[4:5]).astype(BF16)

        def stage():
            x1_scr[...] = live.pop("x1")
            h2_scr[...] = live.pop("h2")

        chunks = range(n_chunks)
        retention = _pipelined([scores_and_cross(c) for c in chunks], [retention_out(c) for c in chunks])
        return retention + [conv, out_proj, norm_ffn], stage

    def finish_ffn():
        xo_ref[...] = x1_scr[...] + modp_ref[5:6] * acc_scr[...]

    @pl.when(i == 0)
    def _():
        phases, stage = mixer_phases()
        _run(phases + [stage])

    @pl.when(jnp.logical_and(i > 0, i < n_tiles))
    def _():
        phases, stage = mixer_phases()
        ffn = _ffn_phases(h2_scr, acc_scr, wg_ref, wu_ref, wd_ref)
        _run(_weave(phases, ffn, [0] + [1] * (len(phases) - 1)) + [finish_ffn, stage])

    @pl.when(i == n_tiles)
    def _():
        _run(_ffn_phases(h2_scr, acc_scr, wg_ref, wu_ref, wd_ref) + [finish_ffn])


def _mix0(x2d, p, dsf, sbp, mod4, tables, state, w, *, n_seq, seq_len, tm, mod_row):
    tiles_per_seq = seq_len // tm
    n_chunks = tm // CHUNK
    has_init = state is not None
    n_tok = n_seq * seq_len
    n_tiles = n_tok // tm
    halo_per_tile = tm // HALO_ROWS
    last_halo = n_tok // HALO_ROWS - 1
    cx_block = P_CX // CONV_WIDTH
    cur = lambda i: jnp.minimum(i, n_tiles - 1)
    prev = lambda i: jnp.maximum(i - 1, 0)
    seq = lambda i: cur(i) // tiles_per_seq

    chunk_spec = pl.BlockSpec((n_chunks, N_HEADS, HEAD_DIM, HEAD_DIM), lambda i: (cur(i), 0, 0, 0))
    in_specs = [
        pl.BlockSpec((tm, D_MODEL), lambda i: (cur(i), 0)),
        pl.BlockSpec((tm, P_COLS), lambda i: (cur(i), 0)),
        pl.BlockSpec((HALO_ROWS, CONV_WIDTH),
                     lambda i: (jnp.maximum(cur(i) * halo_per_tile - 1, 0), cx_block)),
        pl.BlockSpec((HALO_ROWS, CONV_WIDTH),
                     lambda i: (jnp.minimum((cur(i) + 1) * halo_per_tile, last_halo), cx_block)),
        chunk_spec,
        chunk_spec,
    ]
    args = [x2d, p, p, p, dsf, sbp]
    if has_init:
        in_specs.append(_state_spec(seq, 0))
        args.append(state)
    in_specs += [
        pl.BlockSpec((None, None, 6, D_MODEL), lambda i: (0, mod_row(cur(i)), 0, 0)),
        pl.BlockSpec((None, None, 6, D_MODEL), lambda i: (0, mod_row(prev(i)), 0, 0)),
        _resident((N_HEADS, CHUNK, CHUNK)),
        _resident((CHUNK, RET_WIDTH)),
        _resident((CHUNK, RET_WIDTH)),
        _resident((N_HEADS, 1, HEAD_DIM)),
        _resident((1, RET_WIDTH), layer=0),
        _resident((3, CONV_WIDTH), layer=0),
        _resident((1, CONV_WIDTH), layer=0),
        _resident((D_MODEL, D_MODEL), layer=0),
        _resident((1, D_MODEL), layer=0),
        _resident((D_MODEL, FFN_HIDDEN), layer=0),
        _resident((D_MODEL, FFN_HIDDEN), layer=0),
        _resident((FFN_HIDDEN, D_MODEL), layer=0),
    ]
    args += [mod4, mod4, tables["mask"], tables["cross_f"], tables["cross_b"], tables["cd_f"],
             w["ret_norm_g"], w["conv_w"], w["conv_b"], w["w_out_ab"], w["norm_ffn_g"],
             w["w_gate"], w["w_up"], w["w_down"]]
    return pl.pallas_call(
        functools.partial(_mix0_kernel, n_chunks=n_chunks, tiles_per_seq=tiles_per_seq, n_tiles=n_tiles,
                          has_init=has_init),
        grid=(n_tiles + 1,),
        in_specs=in_specs,
        out_specs=[
            pl.BlockSpec((tm, D_MODEL), lambda i: (prev(i), 0)),
            pl.BlockSpec((None, N_HEADS, HEAD_DIM, HEAD_DIM), lambda i: (seq(i), 0, 0, 0)),
        ],
        out_shape=[
            jax.ShapeDtypeStruct((n_tok, D_MODEL), F32),
            jax.ShapeDtypeStruct((n_seq, N_HEADS, HEAD_DIM, HEAD_DIM), F32),
        ],
        scratch_shapes=[pltpu.VMEM((tm, D_MODEL), BF16), pltpu.VMEM((tm, D_MODEL), F32),
                        pltpu.VMEM((tm, D_MODEL), BF16), pltpu.VMEM((tm, D_MODEL), F32)],
        compiler_params=pltpu.CompilerParams(
            dimension_semantics=("arbitrary",), vmem_limit_bytes=VMEM_LIMIT_BYTES),
        name="retention_conv_ffn0",
    )(*args)


def _mix1_kernel(x_ref, mod_ref, modp_ref, gain_ref, win_ref, vg_ref, ws_ref, bs_ref, wout_ref, gain2_ref,
                 wg_ref, wu_ref, wd_ref, gfin_ref, o_ref, gated_scr, x1_scr, h2_scr, acc_scr, *,
                 n_chunks, n_tiles):
    i = pl.program_id(0)

    def mixer_phases():
        live = {}

        def norm_mix():
            live["h"] = _norm_mod(x_ref[...], gain_ref[...], mod_ref[0:1], mod_ref[1:2]).astype(BF16)

        def proj_u():
            live["u"] = _gelu_tanh(_dot(live["h"], win_ref[:, :D_MODEL]))

        def proj_v():
            live["v"] = _gelu_tanh(_dot(live.pop("h"), win_ref[:, D_MODEL:]))

        def norm_v():
            v = live.pop("v")
            live["v"] = (v * lax.rsqrt(jnp.mean(v * v, axis=-1, keepdims=True) + EPS) * vg_ref[...]).astype(BF16)

        def spatial(c):
            def phase():
                rows = slice(c * CHUNK, (c + 1) * CHUNK)
                for g in range(CMLP_GROUPS):
                    cols = slice(g * GROUP_WIDTH, (g + 1) * GROUP_WIDTH)
                    s = _dot(ws_ref[g], live["v"][rows, cols]) + bs_ref[g]
                    gated_scr[rows, cols] = (live["u"][rows, cols] * s).astype(BF16)
            return phase

        def out_proj():
            live["x1"] = x_ref[...] + mod_ref[2:3] * _dot(gated_scr[...], wout_ref[...])

        def norm_ffn():
            live["h2"] = _norm_mod(live["x1"], gain2_ref[...], mod_ref[3:4], mod_ref[4:5]).astype(BF16)

        def stage():
            x1_scr[...] = live.pop("x1")
            h2_scr[...] = live.pop("h2")

        phases = [norm_mix, proj_u, proj_v, norm_v] + [spatial(c) for c in range(n_chunks)]
        return phases + [out_proj, norm_ffn], stage

    def finish_ffn():
        x2 = x1_scr[...] + modp_ref[5:6] * acc_scr[...]
        o_ref[...] = x2 * lax.rsqrt(jnp.mean(x2 * x2, axis=-1, keepdims=True) + EPS) * gfin_ref[...]

    @pl.when(i == 0)
    def _():
        phases, stage = mixer_phases()
        _run(phases + [stage])

    @pl.when(jnp.logical_and(i > 0, i < n_tiles))
    def _():
        phases, stage = mixer_phases()
        ffn = _ffn_phases(h2_scr, acc_scr, wg_ref, wu_ref, wd_ref)
        gaps = [1, 2, 2, 4, 2] + [0] * (n_chunks - 1) + [2, 2]
        _run(_weave(phases, ffn, gaps) + [finish_ffn, stage])

    @pl.when(i == n_tiles)
    def _():
        _run(_ffn_phases(h2_scr, acc_scr, wg_ref, wu_ref, wd_ref) + [finish_ffn])


def _mix1(x2d, mod4, w, *, tm, mod_row):
    n_tok = x2d.shape[0]
    n_chunks = tm // CHUNK
    n_tiles = n_tok // tm
    cur = lambda i: jnp.minimum(i, n_tiles - 1)
    prev = lambda i: jnp.maximum(i - 1, 0)
    in_specs = [
        pl.BlockSpec((tm, D_MODEL), lambda i: (cur(i), 0)),
        pl.BlockSpec((None, None, 6, D_MODEL), lambda i: (1, mod_row(cur(i)), 0, 0)),
        pl.BlockSpec((None, None, 6, D_MODEL), lambda i: (1, mod_row(prev(i)), 0, 0)),
        _resident((1, D_MODEL), layer=1),
        _resident((D_MODEL, 2 * D_MODEL), layer=0),
        _resident((1, D_MODEL), layer=0),
        _resident((CMLP_GROUPS, CHUNK, CHUNK), layer=0),
        _resident((CMLP_GROUPS, CHUNK, 1), layer=0),
        _resident((D_MODEL, D_MODEL), layer=0),
        _resident((1, D_MODEL), layer=1),
        _resident((D_MODEL, FFN_HIDDEN), layer=1),
        _resident((D_MODEL, FFN_HIDDEN), layer=1),
        _resident((FFN_HIDDEN, D_MODEL), layer=1),
        _resident((1, D_MODEL)),
    ]
    args = [x2d, mod4, mod4, w["norm_mix_g"], w["w_in_c"], w["c_norm_g"], w["w_spatial"], w["b_spatial"],
            w["w_out_c"], w["norm_ffn_g"], w["w_gate"], w["w_up"], w["w_down"], w["final_norm_g"]]
    return pl.pallas_call(
        functools.partial(_mix1_kernel, n_chunks=n_chunks, n_tiles=n_tiles),
        grid=(n_tiles + 1,),
        in_specs=in_specs,
        out_specs=pl.BlockSpec((tm, D_MODEL), lambda i: (prev(i), 0)),
        out_shape=jax.ShapeDtypeStruct((n_tok, D_MODEL), F32),
        scratch_shapes=[pltpu.VMEM((tm, D_MODEL), BF16), pltpu.VMEM((tm, D_MODEL), F32),
                        pltpu.VMEM((tm, D_MODEL), BF16), pltpu.VMEM((tm, D_MODEL), F32)],
        compiler_params=pltpu.CompilerParams(
            dimension_semantics=("arbitrary",), vmem_limit_bytes=VMEM_LIMIT_BYTES),
        name="chunkmlp_ffn1",
    )(*args)


def _decay_tables(decay_logit):
    lg = jax.nn.log_sigmoid(decay_logit.astype(F32))
    idx = jnp.arange(CHUNK, dtype=F32)
    diff = idx[:, None] - idx[None, :]
    lg_f, lg_b = lg[0], lg[1]
    mask_f = jnp.where(diff >= 0, jnp.exp(jnp.maximum(diff, 0.0) * lg_f[:, None, None]), 0.0)
    mask_b = jnp.where(diff <= 0, jnp.exp(jnp.maximum(-diff, 0.0) * lg_b[:, None, None]), 0.0)
    per_head = lambda t: jnp.repeat(t.T, HEAD_DIM, axis=1)
    lanes = lambda t: jnp.broadcast_to(t[:, None, None], (N_HEADS, 1, HEAD_DIM))
    return {
        "mask": mask_f + mask_b,
        "cross_f": per_head(jnp.exp((idx + 1.0) * lg_f[:, None])),
        "cross_b": per_head(jnp.exp((CHUNK - idx) * lg_b[:, None])),
        "kdec_f": per_head(jnp.exp((CHUNK - 1.0 - idx) * lg_f[:, None])),
        "kdec_b": per_head(jnp.exp(idx * lg_b[:, None])),
        "cd_f": lanes(jnp.exp(CHUNK * lg_f)),
        "cd_b": lanes(jnp.exp(CHUNK * lg_b)),
    }


def _rope_tables(seq_len):
    rows = seq_len // GRID_W
    row = np.repeat(np.arange(rows, dtype=np.float32), GRID_W)
    col = np.tile(np.arange(GRID_W, dtype=np.float32), rows)
    nf = HEAD_DIM // 4
    freqs = np.float32(ROPE_BASE) ** (-np.arange(nf, dtype=np.float32) / np.float32(nf))
    ang = np.concatenate([row[:, None] * freqs, col[:, None] * freqs], axis=-1).astype(np.float64)
    cos, sin = np.cos(ang).astype(np.float32), np.sin(ang).astype(np.float32)
    return {"cos2": jnp.asarray(np.concatenate([cos, cos], axis=-1)),
            "sin2": jnp.asarray(np.concatenate([-sin, sin], axis=-1))}


def _trunk_pass(x, mod4, state, w, tables, *, use_rope, mod_row_of_seq):
    n_seq, seq_len, _ = x.shape
    tm = min(TOKEN_TILE, seq_len)
    tiles_per_seq = seq_len // tm
    mod_row = lambda t: mod_row_of_seq(t // tiles_per_seq)
    x2d = x.reshape(n_seq * seq_len, D_MODEL)
    p, dsf, sbp, sb_fin = _inproj(x2d, mod4, w, tables, state, n_seq=n_seq, seq_len=seq_len, tm=tm,
                                  use_rope=use_rope, mod_row=mod_row)
    x1, sf_fin = _mix0(x2d, p, dsf, sbp, mod4, tables, state, w, n_seq=n_seq, seq_len=seq_len, tm=tm,
                       mod_row=mod_row)
    y = _mix1(x1, mod4, w, tm=tm, mod_row=mod_row)
    return y.reshape(n_seq, seq_len, D_MODEL), sf_fin, sb_fin


def kernel(x_prompt, x_sample, state_ret, c, c_ctx, ada_w, ada_b, norm_mix_g, norm_ffn_g, w_in_ab,
           ret_decay_logit, ret_norm_g, conv_w, conv_b, w_out_ab, w_in_c, c_norm_g, w_spatial, b_spatial,
           w_out_c, w_gate, w_up, w_down, final_norm_g):
    n_lat = c.shape[0]
    cvecs = jnp.concatenate(
        [c_ctx[None, :], c, jnp.zeros((MOD_ROWS - 1 - n_lat, D_MODEL), F32)], axis=0)
    mod4 = _modulation(cvecs, ada_w, ada_b)

    rows = lambda a: a[..., None, :]
    w = {
        "norm_mix_g": rows(norm_mix_g), "norm_ffn_g": rows(norm_ffn_g),
        "w_in_ab": w_in_ab.astype(BF16), "ret_norm_g": rows(ret_norm_g),
        "conv_w": conv_w, "conv_b": rows(conv_b), "w_out_ab": w_out_ab.astype(BF16),
        "w_in_c": w_in_c.astype(BF16), "c_norm_g": rows(c_norm_g),
        "w_spatial": w_spatial.astype(BF16), "b_spatial": b_spatial[..., None],
        "w_out_c": w_out_c.astype(BF16),
        "w_gate": w_gate.astype(BF16), "w_up": w_up.astype(BF16), "w_down": w_down.astype(BF16),
        "final_norm_g": final_norm_g.reshape(1, -1),
    }
    tables = _decay_tables(ret_decay_logit[0])

    y_prompt, sf_ctx, sb_ctx = _trunk_pass(x_prompt, mod4, None, w, tables, use_rope=False,
                                           mod_row_of_seq=lambda b: 0)
    lat_tables = dict(tables, **_rope_tables(x_sample.shape[1]))
    y_sample, _, _ = _trunk_pass(x_sample, mod4, state_ret, w, lat_tables, use_rope=True,
                                 mod_row_of_seq=lambda b: b + 1)
    new_state = jnp.stack([sf_ctx, sb_ctx], axis=1)[:, None]
    return (y_prompt, y_sample, new_state)
```

```python
import functools
import math

import numpy as np
import jax
import jax.numpy as jnp
from jax import lax
from jax.experimental import pallas as pl
from jax.experimental.pallas import tpu as pltpu

D_MODEL = 1024
N_HEADS = 4
HEAD_DIM = 128
RET_WIDTH = N_HEADS * HEAD_DIM
CONV_WIDTH = 512
CHUNK = 128
CMLP_GROUPS = 4
GROUP_WIDTH = D_MODEL // CMLP_GROUPS
FFN_HIDDEN = 2816
AB_IN = 4 * RET_WIDTH + 3 * CONV_WIDTH
GRID_W = 64
ROPE_BASE = 10000.0
EPS = 1e-6
MOD_ROWS = 16
MOD_COLS_PER_STEP = 1536

P_Q, P_K, P_V, P_G, P_BG, P_CX = 0, 512, 1024, 1536, 2048, 2560
P_COLS = 3072
HALO_ROWS = 16

MXU_TILE = 256
TOKEN_TILE = 512
FFN_SLABS = tuple((lo, lo + MXU_TILE) for lo in range(0, FFN_HIDDEN, MXU_TILE))
VMEM_LIMIT_BYTES = 56 * 1024 * 1024

F32 = jnp.float32
BF16 = jnp.bfloat16


def _dot(a, b):
    return jnp.dot(a, b, preferred_element_type=F32)


def _resident(shape, layer=None):
    if layer is None:
        zeros = (0,) * len(shape)
        return pl.BlockSpec(shape, lambda *_: zeros, pipeline_mode=pl.Buffered(1))
    index = (layer,) + (0,) * len(shape)
    return pl.BlockSpec((None,) + tuple(shape), lambda *_: index, pipeline_mode=pl.Buffered(1))


def _norm_mod(x, gain, shift, scale):
    y = x * lax.rsqrt(jnp.mean(x * x, axis=-1, keepdims=True) + EPS)
    return (y * gain) * (1.0 + scale) + shift


def _silu(x):
    hx = 0.5 * x
    return hx + hx * jnp.tanh(hx)


def _gelu_tanh(x):
    return 0.5 * x * (1.0 + jnp.tanh(math.sqrt(2.0 / math.pi) * (x + 0.044715 * (x * x * x))))


def _pipelined(first_halves, second_halves):
    order = [first_halves[0]]
    for k in range(1, len(first_halves)):
        order += [first_halves[k], second_halves[k - 1]]
    order.append(second_halves[-1])
    return order


def _weave(primary, filler, gaps):
    filler = list(filler)
    order = []
    for phase, gap in zip(primary, gaps):
        order += filler[:gap] + [phase]
        filler = filler[gap:]
    return order + filler


def _run(phases):
    for phase in phases:
        phase()


def _ffn_phases(h2_scr, acc_scr, wg_ref, wu_ref, wd_ref):
    acts = {}

    def up(k):
        def phase():
            lo, hi = FFN_SLABS[k]
            h2 = h2_scr[...]
            acts[k] = (_silu(_dot(h2, wg_ref[:, lo:hi])) * _dot(h2, wu_ref[:, lo:hi])).astype(BF16)
        return phase

    def down(k):
        def phase():
            lo, hi = FFN_SLABS[k]
            part = _dot(acts.pop(k), wd_ref[lo:hi, :])
            if k == 0:
                acc_scr[...] = part
            else:
                acc_scr[...] += part
        return phase

    slabs = range(len(FFN_SLABS))
    return _pipelined([up(k) for k in slabs], [down(k) for k in slabs])


def _mod_kernel(cv_ref, w_ref, b_ref, o_ref):
    sc = _silu(cv_ref[...]).astype(BF16)
    o_ref[...] = _dot(sc, w_ref[...].astype(BF16)) + b_ref[...]


def _modulation(cvecs, ada_w, ada_b):
    depth, _, n_mod = ada_w.shape
    steps = n_mod // MOD_COLS_PER_STEP
    out = pl.pallas_call(
        _mod_kernel,
        grid=(depth, steps),
        in_specs=[
            pl.BlockSpec((MOD_ROWS, D_MODEL), lambda l, n: (0, 0)),
            pl.BlockSpec((None, D_MODEL, MOD_COLS_PER_STEP), lambda l, n: (l, 0, n)),
            pl.BlockSpec((None, 1, MOD_COLS_PER_STEP), lambda l, n: (l, 0, n)),
        ],
        out_specs=pl.BlockSpec((None, MOD_ROWS, MOD_COLS_PER_STEP), lambda l, n: (l, 0, n)),
        out_shape=jax.ShapeDtypeStruct((depth, MOD_ROWS, n_mod), F32),
        compiler_params=pltpu.CompilerParams(
            dimension_semantics=("arbitrary", "arbitrary"), vmem_limit_bytes=VMEM_LIMIT_BYTES),
        name="adaln_modulation",
    )(cvecs, ada_w, ada_b.reshape(depth, 1, n_mod))
    return out.reshape(depth, MOD_ROWS, 6, D_MODEL)


def _inproj_kernel(*refs, n_chunks, use_rope, has_init):
    it = iter(refs)
    x_ref, mod_ref, gain_ref, w_ref = next(it), next(it), next(it), next(it)
    cos_ref = sin_ref = s0_ref = None
    if use_rope:
        cos_ref, sin_ref = next(it), next(it)
    kdf_ref, kdb_ref, cdb_ref = next(it), next(it), next(it)
    if has_init:
        s0_ref = next(it)
    p_ref, dsf_ref, sbp_ref, sb_ref = next(it), next(it), next(it), next(it)

    @pl.when(pl.program_id(1) == 0)
    def _():
        if has_init:
            sb_ref[...] = s0_ref[...]
        else:
            sb_ref[...] = jnp.zeros_like(sb_ref)

    mod = mod_ref[...]
    h = _norm_mod(x_ref[...], gain_ref[...], mod[0:1], mod[1:2]).astype(BF16)
    p = _dot(h, w_ref[...])

    q = p[:, 0:RET_WIDTH] * (HEAD_DIM ** -0.5)
    k = p[:, RET_WIDTH:2 * RET_WIDTH]
    v = p[:, 2 * RET_WIDTH:3 * RET_WIDTH]
    if use_rope:
        cos2, sin2 = cos_ref[...], sin_ref[...]

        def rope(a):
            heads = []
            for hd in range(N_HEADS):
                ah = a[:, hd * HEAD_DIM:(hd + 1) * HEAD_DIM]
                heads.append(ah * cos2 + pltpu.roll(ah, HEAD_DIM // 2, 1) * sin2)
            return jnp.concatenate(heads, axis=1)

        q, k = rope(q), rope(k)

    p_ref[:, P_Q:P_Q + RET_WIDTH] = q.astype(BF16)
    p_ref[:, P_K:P_K + RET_WIDTH] = k.astype(BF16)
    v16 = v.astype(BF16)
    p_ref[:, P_V:P_V + RET_WIDTH] = v16
    p_ref[:, P_G:P_BG + CONV_WIDTH] = p[:, 3 * RET_WIDTH:4 * RET_WIDTH + CONV_WIDTH].astype(BF16)
    cg = p[:, 4 * RET_WIDTH + CONV_WIDTH:4 * RET_WIDTH + 2 * CONV_WIDTH]
    xc = p[:, 4 * RET_WIDTH + 2 * CONV_WIDTH:AB_IN]
    p_ref[:, P_CX:P_COLS] = (cg * xc).astype(BF16)

    tn = (((0,), (0,)), ((), ()))
    for c in reversed(range(n_chunks)):
        rows = slice(c * CHUNK, (c + 1) * CHUNK)
        kc = k[rows]
        kf = (kc * kdf_ref[...]).astype(BF16)
        kb = (kc * kdb_ref[...]).astype(BF16)
        vc = v16[rows]
        for hd in range(N_HEADS):
            cols = slice(hd * HEAD_DIM, (hd + 1) * HEAD_DIM)
            dsf_ref[c, hd] = lax.dot_general(kf[:, cols], vc[:, cols], tn, preferred_element_type=F32)
            dsb = lax.dot_general(kb[:, cols], vc[:, cols], tn, preferred_element_type=F32)
            sb = sb_ref[hd]
            sbp_ref[c, hd] = sb.astype(BF16)
            sb_ref[hd] = sb * cdb_ref[hd] + dsb


def _state_spec(index_of_seq, direction):
    return pl.BlockSpec((None, None, None, N_HEADS, HEAD_DIM, HEAD_DIM),
                        lambda *g: (index_of_seq(*g), 0, direction, 0, 0, 0))


def _inproj(x2d, mod4, w, tables, state, *, n_seq, seq_len, tm, use_rope, mod_row):
    nt = seq_len // tm
    n_chunks = tm // CHUNK
    has_init = state is not None
    tile = lambda b, j: b * nt + (nt - 1 - j)

    in_specs = [
        pl.BlockSpec((tm, D_MODEL), lambda b, j: (tile(b, j), 0)),
        pl.BlockSpec((None, None, 6, D_MODEL), lambda b, j: (0, mod_row(tile(b, j)), 0, 0)),
        _resident((1, D_MODEL), layer=0),
        _resident((D_MODEL, AB_IN), layer=0),
    ]
    args = [x2d, mod4, w["norm_mix_g"], w["w_in_ab"]]
    if use_rope:
        in_specs += [pl.BlockSpec((tm, HEAD_DIM), lambda b, j: (nt - 1 - j, 0))] * 2
        args += [tables["cos2"], tables["sin2"]]
    in_specs += [
        pl.BlockSpec((CHUNK, RET_WIDTH), lambda b, j: (0, 0)),
        pl.BlockSpec((CHUNK, RET_WIDTH), lambda b, j: (0, 0)),
        pl.BlockSpec((N_HEADS, 1, HEAD_DIM), lambda b, j: (0, 0, 0)),
    ]
    args += [tables["kdec_f"], tables["kdec_b"], tables["cd_b"]]
    if has_init:
        in_specs.append(_state_spec(lambda b, j: b, 1))
        args.append(state)

    n_tok = n_seq * seq_len
    chunk_spec = pl.BlockSpec((n_chunks, N_HEADS, HEAD_DIM, HEAD_DIM), lambda b, j: (tile(b, j), 0, 0, 0))
    chunk_shape = (n_tok // CHUNK, N_HEADS, HEAD_DIM, HEAD_DIM)
    return pl.pallas_call(
        functools.partial(_inproj_kernel, n_chunks=n_chunks, use_rope=use_rope, has_init=has_init),
        grid=(n_seq, nt),
        in_specs=in_specs,
        out_specs=[
            pl.BlockSpec((tm, P_COLS), lambda b, j: (tile(b, j), 0)),
            chunk_spec,
            chunk_spec,
            pl.BlockSpec((None, N_HEADS, HEAD_DIM, HEAD_DIM), lambda b, j: (b, 0, 0, 0)),
        ],
        out_shape=[
            jax.ShapeDtypeStruct((n_tok, P_COLS), BF16),
            jax.ShapeDtypeStruct(chunk_shape, F32),
            jax.ShapeDtypeStruct(chunk_shape, BF16),
            jax.ShapeDtypeStruct((n_seq, N_HEADS, HEAD_DIM, HEAD_DIM), F32),
        ],
        compiler_params=pltpu.CompilerParams(
            dimension_semantics=("arbitrary", "arbitrary"), vmem_limit_bytes=VMEM_LIMIT_BYTES),
        name="inproj_bwd_scan",
    )(*args)


def _mix0_kernel(*refs, n_chunks, tiles_per_seq, n_tiles, has_init):
    it = iter(refs)
    x_ref, p_ref, hprev_ref, hnext_ref, dsf_ref, sbp_ref = (next(it) for _ in range(6))
    s0_ref = next(it) if has_init else None
    (mod_ref, modp_ref, mask_ref, crf_ref, crb_ref, cdf_ref, retg_ref, convw_ref, convb_ref, wout_ref,
     gain_ref, wg_ref, wu_ref, wd_ref) = (next(it) for _ in range(14))
    xo_ref, sf_ref = next(it), next(it)
    y_scr, x1_scr, h2_scr, acc_scr = (next(it) for _ in range(4))

    i = pl.program_id(0)
    j = lax.rem(i, tiles_per_seq)
    tm = x_ref.shape[0]

    @pl.when(jnp.logical_and(j == 0, i < n_tiles))
    def _():
        if has_init:
            sf_ref[...] = s0_ref[...]
        else:
            sf_ref[...] = jnp.zeros_like(sf_ref)

    def mixer_phases():
        nt_dims = (((1,), (1,)), ((), ()))
        live = {}

        def head_cols(base, hd):
            return slice(base + hd * HEAD_DIM, base + (hd + 1) * HEAD_DIM)

        def scores_and_cross(c):
            def phase():
                rows = slice(c * CHUNK, (c + 1) * CHUNK)
                for hd in range(N_HEADS):
                    cols = head_cols(0, hd)
                    qh = p_ref[rows, head_cols(P_Q, hd)]
                    kh = p_ref[rows, head_cols(P_K, hd)]
                    scores = lax.dot_general(qh, kh, nt_dims, preferred_element_type=F32) * mask_ref[hd]
                    sf = sf_ref[hd]
                    states = jnp.concatenate([sf.astype(BF16), sbp_ref[c, hd]], axis=1)
                    qs = _dot(qh, states)
                    cross = qs[:, :HEAD_DIM] * crf_ref[:, cols] + qs[:, HEAD_DIM:] * crb_ref[:, cols]
                    sf_ref[hd] = sf * cdf_ref[hd] + dsf_ref[c, hd]
                    live[c, hd] = (scores.astype(BF16), cross)
            return phase

        def retention_out(c):
            def phase():
                rows = slice(c * CHUNK, (c + 1) * CHUNK)
                for hd in range(N_HEADS):
                    cols = head_cols(0, hd)
                    scores, cross = live.pop((c, hd))
                    o = _dot(scores, p_ref[rows, head_cols(P_V, hd)]) + cross
                    o = o * lax.rsqrt(jnp.mean(o * o, axis=-1, keepdims=True) + EPS) * retg_ref[:, cols]
                    gate = p_ref[rows, head_cols(P_G, hd)].astype(F32)
                    y_scr[rows, cols] = (_silu(gate) * o).astype(BF16)
            return phase

        def conv():
            cx = p_ref[:, P_CX:P_COLS].astype(F32)
            row_id = lax.broadcasted_iota(jnp.int32, cx.shape, 0)
            prev_row = hprev_ref[...].astype(F32)[HALO_ROWS - 1:HALO_ROWS] * jnp.where(j > 0, 1.0, 0.0)
            next_row = hnext_ref[...].astype(F32)[0:1] * jnp.where(j < tiles_per_seq - 1, 1.0, 0.0)
            prev = jnp.where(row_id == 0, prev_row, pltpu.roll(cx, 1, 0))
            nxt = jnp.where(row_id == tm - 1, next_row, pltpu.roll(cx, tm - 1, 0))
            out = convw_ref[0:1] * prev + convw_ref[1:2] * cx + convw_ref[2:3] * nxt + convb_ref[...]
            y_scr[:, RET_WIDTH:] = (p_ref[:, P_BG:P_BG + CONV_WIDTH].astype(F32) * out).astype(BF16)

        def out_proj():
            live["x1"] = x_ref[...] + mod_ref[2:3] * _dot(y_scr[...], wout_ref[...])

        def norm_ffn():
            live["h2"] = _norm_mod(live["x1"], gain_ref[...], mod_ref[3:4], mod_ref[4:5]).astype(BF16)

        def stage():
            x1_scr[...] = live.pop("x1")
            h2_scr[...] = live.pop("h2")

        chunks = range(n_chunks)
        retention = _pipelined([scores_and_cross(c) for c in chunks], [retention_out(c) for c in chunks])
        return retention + [conv, out_proj, norm_ffn], stage

    def finish_ffn():
        xo_ref[...] = x1_scr[...] + modp_ref[5:6] * acc_scr[...]

    @pl.when(i == 0)
    def _():
        phases, stage = mixer_phases()
        _run(phases + [stage])

    @pl.when(jnp.logical_and(i > 0, i < n_tiles))
    def _():
        phases, stage = mixer_phases()
        ffn = _ffn_phases(h2_scr, acc_scr, wg_ref, wu_ref, wd_ref)
        _run(_weave(phases, ffn, [0] + [1] * (len(phases) - 1)) + [finish_ffn, stage])

    @pl.when(i == n_tiles)
    def _():
        _run(_ffn_phases(h2_scr, acc_scr, wg_ref, wu_ref, wd_ref) + [finish_ffn])


def _mix0(x2d, p, dsf, sbp, mod4, tables, state, w, *, n_seq, seq_len, tm, mod_row):
    tiles_per_seq = seq_len // tm
    n_chunks = tm // CHUNK
    has_init = state is not None
    n_tok = n_seq * seq_len
    n_tiles = n_tok // tm
    halo_per_tile = tm // HALO_ROWS
    last_halo = n_tok // HALO_ROWS - 1
    cx_block = P_CX // CONV_WIDTH
    cur = lambda i: jnp.minimum(i, n_tiles - 1)
    prev = lambda i: jnp.maximum(i - 1, 0)
    seq = lambda i: cur(i) // tiles_per_seq

    chunk_spec = pl.BlockSpec((n_chunks, N_HEADS, HEAD_DIM, HEAD_DIM), lambda i: (cur(i), 0, 0, 0))
    in_specs = [
        pl.BlockSpec((tm, D_MODEL), lambda i: (cur(i), 0)),
        pl.BlockSpec((tm, P_COLS), lambda i: (cur(i), 0)),
        pl.BlockSpec((HALO_ROWS, CONV_WIDTH),
                     lambda i: (jnp.maximum(cur(i) * halo_per_tile - 1, 0), cx_block)),
        pl.BlockSpec((HALO_ROWS, CONV_WIDTH),
                     lambda i: (jnp.minimum((cur(i) + 1) * halo_per_tile, last_halo), cx_block)),
        chunk_spec,
        chunk_spec,
    ]
    args = [x2d, p, p, p, dsf, sbp]
    if has_init:
        in_specs.append(_state_spec(seq, 0))
        args.append(state)
    in_specs += [
        pl.BlockSpec((None, None, 6, D_MODEL), lambda i: (0, mod_row(cur(i)), 0, 0)),
        pl.BlockSpec((None, None, 6, D_MODEL), lambda i: (0, mod_row(prev(i)), 0, 0)),
        _resident((N_HEADS, CHUNK, CHUNK)),
        _resident((CHUNK, RET_WIDTH)),
        _resident((CHUNK, RET_WIDTH)),
        _resident((N_HEADS, 1, HEAD_DIM)),
        _resident((1, RET_WIDTH), layer=0),
        _resident((3, CONV_WIDTH), layer=0),
        _resident((1, CONV_WIDTH), layer=0),
        _resident((D_MODEL, D_MODEL), layer=0),
        _resident((1, D_MODEL), layer=0),
        _resident((D_MODEL, FFN_HIDDEN), layer=0),
        _resident((D_MODEL, FFN_HIDDEN), layer=0),
        _resident((FFN_HIDDEN, D_MODEL), layer=0),
    ]
    args += [mod4, mod4, tables["mask"], tables["cross_f"], tables["cross_b"], tables["cd_f"],
             w["ret_norm_g"], w["conv_w"], w["conv_b"], w["w_out_ab"], w["norm_ffn_g"],
             w["w_gate"], w["w_up"], w["w_down"]]
    return pl.pallas_call(
        functools.partial(_mix0_kernel, n_chunks=n_chunks, tiles_per_seq=tiles_per_seq, n_tiles=n_tiles,
                          has_init=has_init),
        grid=(n_tiles + 1,),
        in_specs=in_specs,
        out_specs=[
            pl.BlockSpec((tm, D_MODEL), lambda i: (prev(i), 0)),
            pl.BlockSpec((None, N_HEADS, HEAD_DIM, HEAD_DIM), lambda i: (seq(i), 0, 0, 0)),
        ],
        out_shape=[
            jax.ShapeDtypeStruct((n_tok, D_MODEL), F32),
            jax.ShapeDtypeStruct((n_seq, N_HEADS, HEAD_DIM, HEAD_DIM), F32),
        ],
        scratch_shapes=[pltpu.VMEM((tm, D_MODEL), BF16), pltpu.VMEM((tm, D_MODEL), F32),
                        pltpu.VMEM((tm, D_MODEL), BF16), pltpu.VMEM((tm, D_MODEL), F32)],
        compiler_params=pltpu.CompilerParams(
            dimension_semantics=("arbitrary",), vmem_limit_bytes=VMEM_LIMIT_BYTES),
        name="retention_conv_ffn0",
    )(*args)


def _mix1_kernel(x_ref, mod_ref, modp_ref, gain_ref, win_ref, vg_ref, ws_ref, bs_ref, wout_ref, gain2_ref,
                 wg_ref, wu_ref, wd_ref, gfin_ref, o_ref, gated_scr, x1_scr, h2_scr, acc_scr, *,
                 n_chunks, n_tiles):
    i = pl.program_id(0)

    def mixer_phases():
        live = {}

        def norm_mix():
            live["h"] = _norm_mod(x_ref[...], gain_ref[...], mod_ref[0:1], mod_ref[1:2]).astype(BF16)

        def proj_u():
            live["u"] = _gelu_tanh(_dot(live["h"], win_ref[:, :D_MODEL]))

        def proj_v():
            live["v"] = _gelu_tanh(_dot(live.pop("h"), win_ref[:, D_MODEL:]))

        def norm_v():
            v = live.pop("v")
            live["v"] = (v * lax.rsqrt(jnp.mean(v * v, axis=-1, keepdims=True) + EPS) * vg_ref[...]).astype(BF16)

        def spatial(c):
            def phase():
                rows = slice(c * CHUNK, (c + 1) * CHUNK)
                for g in range(CMLP_GROUPS):
                    cols = slice(g * GROUP_WIDTH, (g + 1) * GROUP_WIDTH)
                    s = _dot(ws_ref[g], live["v"][rows, cols]) + bs_ref[g]
                    gated_scr[rows, cols] = (live["u"][rows, cols] * s).astype(BF16)
            return phase

        def out_proj():
            live["x1"] = x_ref[...] + mod_ref[2:3] * _dot(gated_scr[...], wout_ref[...])

        def norm_ffn():
            live["h2"] = _norm_mod(live["x1"], gain2_ref[...], mod_ref[3:4], mod_ref[4:5]).astype(BF16)

        def stage():
            x1_scr[...] = live.pop("x1")
            h2_scr[...] = live.pop("h2")

        phases = [norm_mix, proj_u, proj_v, norm_v] + [spatial(c) for c in range(n_chunks)]
        return phases + [out_proj, norm_ffn], stage

    def finish_ffn():
        x2 = x1_scr[...] + modp_ref[5:6] * acc_scr[...]
        o_ref[...] = x2 * lax.rsqrt(jnp.mean(x2 * x2, axis=-1, keepdims=True) + EPS) * gfin_ref[...]

    @pl.when(i == 0)
    def _():
        phases, stage = mixer_phases()
        _run(phases + [stage])

    @pl.when(jnp.logical_and(i > 0, i < n_tiles))
    def _():
        phases, stage = mixer_phases()
        ffn = _ffn_phases(h2_scr, acc_scr, wg_ref, wu_ref, wd_ref)
        gaps = [1, 2, 2, 4, 2] + [0] * (n_chunks - 1) + [2, 2]
        _run(_weave(phases, ffn, gaps) + [finish_ffn, stage])

    @pl.when(i == n_tiles)
    def _():
        _run(_ffn_phases(h2_scr, acc_scr, wg_ref, wu_ref, wd_ref) + [finish_ffn])


def _mix1(x2d, mod4, w, *, tm, mod_row):
    n_tok = x2d.shape[0]
    n_chunks = tm // CHUNK
    n_tiles = n_tok // tm
    cur = lambda i: jnp.minimum(i, n_tiles - 1)
    prev = lambda i: jnp.maximum(i - 1, 0)
    in_specs = [
        pl.BlockSpec((tm, D_MODEL), lambda i: (cur(i), 0)),
        pl.BlockSpec((None, None, 6, D_MODEL), lambda i: (1, mod_row(cur(i)), 0, 0)),
        pl.BlockSpec((None, None, 6, D_MODEL), lambda i: (1, mod_row(prev(i)), 0, 0)),
        _resident((1, D_MODEL), layer=1),
        _resident((D_MODEL, 2 * D_MODEL), layer=0),
        _resident((1, D_MODEL), layer=0),
        _resident((CMLP_GROUPS, CHUNK, CHUNK), layer=0),
        _resident((CMLP_GROUPS, CHUNK, 1), layer=0),
        _resident((D_MODEL, D_MODEL), layer=0),
        _resident((1, D_MODEL), layer=1),
        _resident((D_MODEL, FFN_HIDDEN), layer=1),
        _resident((D_MODEL, FFN_HIDDEN), layer=1),
        _resident((FFN_HIDDEN, D_MODEL), layer=1),
        _resident((1, D_MODEL)),
    ]
    args = [x2d, mod4, mod4, w["norm_mix_g"], w["w_in_c"], w["c_norm_g"], w["w_spatial"], w["b_spatial"],
            w["w_out_c"], w["norm_ffn_g"], w["w_gate"], w["w_up"], w["w_down"], w["final_norm_g"]]
    return pl.pallas_call(
        functools.partial(_mix1_kernel, n_chunks=n_chunks, n_tiles=n_tiles),
        grid=(n_tiles + 1,),
        in_specs=in_specs,
        out_specs=pl.BlockSpec((tm, D_MODEL), lambda i: (prev(i), 0)),
        out_shape=jax.ShapeDtypeStruct((n_tok, D_MODEL), F32),
        scratch_shapes=[pltpu.VMEM((tm, D_MODEL), BF16), pltpu.VMEM((tm, D_MODEL), F32),
                        pltpu.VMEM((tm, D_MODEL), BF16), pltpu.VMEM((tm, D_MODEL), F32)],
        compiler_params=pltpu.CompilerParams(
            dimension_semantics=("arbitrary",), vmem_limit_bytes=VMEM_LIMIT_BYTES),
        name="chunkmlp_ffn1",
    )(*args)


def _decay_tables(decay_logit):
    lg = jax.nn.log_sigmoid(decay_logit.astype(F32))
    idx = jnp.arange(CHUNK, dtype=F32)
    diff = idx[:, None] - idx[None, :]
    lg_f, lg_b = lg[0], lg[1]
    mask_f = jnp.where(diff >= 0, jnp.exp(jnp.maximum(diff, 0.0) * lg_f[:, None, None]), 0.0)
    mask_b = jnp.where(diff <= 0, jnp.exp(jnp.maximum(-diff, 0.0) * lg_b[:, None, None]), 0.0)
    per_head = lambda t: jnp.repeat(t.T, HEAD_DIM, axis=1)
    lanes = lambda t: jnp.broadcast_to(t[:, None, None], (N_HEADS, 1, HEAD_DIM))
    return {
        "mask": mask_f + mask_b,
        "cross_f": per_head(jnp.exp((idx + 1.0) * lg_f[:, None])),
        "cross_b": per_head(jnp.exp((CHUNK - idx) * lg_b[:, None])),
        "kdec_f": per_head(jnp.exp((CHUNK - 1.0 - idx) * lg_f[:, None])),
        "kdec_b": per_head(jnp.exp(idx * lg_b[:, None])),
        "cd_f": lanes(jnp.exp(CHUNK * lg_f)),
        "cd_b": lanes(jnp.exp(CHUNK * lg_b)),
    }


def _rope_tables(seq_len):
    rows = seq_len // GRID_W
    row = np.repeat(np.arange(rows, dtype=np.float32), GRID_W)
    col = np.tile(np.arange(GRID_W, dtype=np.float32), rows)
    nf = HEAD_DIM // 4
    freqs = np.float32(ROPE_BASE) ** (-np.arange(nf, dtype=np.float32) / np.float32(nf))
    ang = np.concatenate([row[:, None] * freqs, col[:, None] * freqs], axis=-1).astype(np.float64)
    cos, sin = np.cos(ang).astype(np.float32), np.sin(ang).astype(np.float32)
    return {"cos2": jnp.asarray(np.concatenate([cos, cos], axis=-1)),
            "sin2": jnp.asarray(np.concatenate([-sin, sin], axis=-1))}


def _trunk_pass(x, mod4, state, w, tables, *, use_rope, mod_row_of_seq):
    n_seq, seq_len, _ = x.shape
    tm = min(TOKEN_TILE, seq_len)
    tiles_per_seq = seq_len // tm
    mod_row = lambda t: mod_row_of_seq(t // tiles_per_seq)
    x2d = x.reshape(n_seq * seq_len, D_MODEL)
    p, dsf, sbp, sb_fin = _inproj(x2d, mod4, w, tables, state, n_seq=n_seq, seq_len=seq_len, tm=tm,
                                  use_rope=use_rope, mod_row=mod_row)
    x1, sf_fin = _mix0(x2d, p, dsf, sbp, mod4, tables, state, w, n_seq=n_seq, seq_len=seq_len, tm=tm,
                       mod_row=mod_row)
    y = _mix1(x1, mod4, w, tm=tm, mod_row=mod_row)
    return y.reshape(n_seq, seq_len, D_MODEL), sf_fin, sb_fin


def kernel(x_prompt, x_sample, state_ret, c, c_ctx, ada_w, ada_b, norm_mix_g, norm_ffn_g, w_in_ab,
           ret_decay_logit, ret_norm_g, conv_w, conv_b, w_out_ab, w_in_c, c_norm_g, w_spatial, b_spatial,
           w_out_c, w_gate, w_up, w_down, final_norm_g):
    n_lat = c.shape[0]
    cvecs = jnp.concatenate(
        [c_ctx[None, :], c, jnp.zeros((MOD_ROWS - 1 - n_lat, D_MODEL), F32)], axis=0)
    mod4 = _modulation(cvecs, ada_w, ada_b)

    rows = lambda a: a[..., None, :]
    w = {
        "norm_mix_g": rows(norm_mix_g), "norm_ffn_g": rows(norm_ffn_g),
        "w_in_ab": w_in_ab.astype(BF16), "ret_norm_g": rows(ret_norm_g),
        "conv_w": conv_w, "conv_b": rows(conv_b), "w_out_ab": w_out_ab.astype(BF16),
        "w_in_c": w_in_c.astype(BF16), "c_norm_g": rows(c_norm_g),
        "w_spatial": w_spatial.astype(BF16), "b_spatial": b_spatial[..., None],
        "w_out_c": w_out_c.astype(BF16),
        "w_gate": w_gate.astype(BF16), "w_up": w_up.astype(BF16), "w_down": w_down.astype(BF16),
        "final_norm_g": final_norm_g.reshape(1, -1),
    }
    tables = _decay_tables(ret_decay_logit[0])

    y_prompt, sf_ctx, sb_ctx = _trunk_pass(x_prompt, mod4, None, w, tables, use_rope=False,
                                           mod_row_of_seq=lambda b: 0)
    lat_tables = dict(tables, **_rope_tables(x_sample.shape[1]))
    y_sample, _, _ = _trunk_pass(x_sample, mod4, state_ret, w, lat_tables, use_rope=True,
                                 mod_row_of_seq=lambda b: b + 1)
    new_state = jnp.stack([sf_ctx, sb_ctx], axis=1)[:, None]
    return (y_prompt, y_sample, new_state)
```

```python
import functools
import math

import numpy as np
import jax
import jax.numpy as jnp
from jax import lax
from jax.experimental import pallas as pl
from jax.experimental.pallas import tpu as pltpu

D_MODEL = 1024
N_HEADS = 4
HEAD_DIM = 128
RET_WIDTH = N_HEADS * HEAD_DIM
CONV_WIDTH = 512
CHUNK = 128
CMLP_GROUPS = 4
GROUP_WIDTH = D_MODEL // CMLP_GROUPS
FFN_HIDDEN = 2816
AB_IN = 4 * RET_WIDTH + 3 * CONV_WIDTH
GRID_W = 64
ROPE_BASE = 10000.0
EPS = 1e-6
MOD_ROWS = 16
MOD_COLS_PER_STEP = 1536

P_Q, P_K, P_V, P_G, P_BG, P_CX = 0, 512, 1024, 1536, 2048, 2560
P_COLS = 3072
BF16_SUBLANES = 16
HALO_ROWS = BF16_SUBLANES

MXU_TILE = 256
TOKEN_TILE = 512
FFN_SLABS = tuple((lo, lo + MXU_TILE) for lo in range(0, FFN_HIDDEN, MXU_TILE))
VMEM_LIMIT_BYTES = 56 * 1024 * 1024
F32 = jnp.float32
BF16 = jnp.bfloat16


def _dot(a, b):
    return jnp.dot(a, b, preferred_element_type=F32)


def _resident(shape, layer=None):
    if layer is None:
        zeros = (0,) * len(shape)
        return pl.BlockSpec(shape, lambda *_: zeros, pipeline_mode=pl.Buffered(1))
    index = (layer,) + (0,) * len(shape)
    return pl.BlockSpec((None,) + tuple(shape), lambda *_: index, pipeline_mode=pl.Buffered(1))


def _norm_mod(x, gain, shift, scale):
    y = x * lax.rsqrt(jnp.mean(x * x, axis=-1, keepdims=True) + EPS)
    return (y * gain) * (1.0 + scale) + shift


def _silu(x):
    hx = 0.5 * x
    return hx + hx * jnp.tanh(hx)


def _gelu_tanh(x):
    return 0.5 * x * (1.0 + jnp.tanh(math.sqrt(2.0 / math.pi) * (x + 0.044715 * (x * x * x))))


def _pipelined(first_halves, second_halves):
    order = [first_halves[0]]
    for k in range(1, len(first_halves)):
        order += [first_halves[k], second_halves[k - 1]]
    order.append(second_halves[-1])
    return order


def _weave(primary, filler, gaps):
    filler = list(filler)
    order = []
    for phase, gap in zip(primary, gaps):
        order += filler[:gap] + [phase]
        filler = filler[gap:]
    return order + filler


def _run(phases):
    for phase in phases:
        phase()


def _cast_rider(w3d, layer, max_steps, step_of):
    _, rows, cols = w3d.shape
    n_steps = max(n for n in range(1, max_steps + 1)
                  if rows % n == 0 and (rows // n) % BF16_SUBLANES == 0)
    block_rows = rows // n_steps
    blk = lambda *g: jnp.minimum(step_of(*g), n_steps - 1)
    return (pl.BlockSpec((None, block_rows, cols), lambda *g: (layer, blk(*g), 0)),
            pl.BlockSpec((block_rows, cols), lambda *g: (blk(*g), 0)),
            jax.ShapeDtypeStruct((rows, cols), BF16), n_steps)


def _cast_blocks(step, rider_steps, srcs, dsts):
    for n_steps, src, dst in zip(rider_steps, srcs, dsts):
        @pl.when(step < n_steps)
        def _(src=src, dst=dst):
            dst[...] = src[...].astype(BF16)


def _ffn_phases(h2_scr, acc_scr, wg_ref, wu_ref, wd_ref):
    acts = {}

    def up(k):
        def phase():
            lo, hi = FFN_SLABS[k]
            h2 = h2_scr[...]
            acts[k] = (_silu(_dot(h2, wg_ref[:, lo:hi])) * _dot(h2, wu_ref[:, lo:hi])).astype(BF16)
        return phase

    def down(k):
        def phase():
            lo, hi = FFN_SLABS[k]
            part = _dot(acts.pop(k), wd_ref[lo:hi, :])
            if k == 0:
                acc_scr[...] = part
            else:
                acc_scr[...] += part
        return phase

    slabs = range(len(FFN_SLABS))
    return _pipelined([up(k) for k in slabs], [down(k) for k in slabs])


def _mod_kernel(cv_ref, w_ref, b_ref, o_ref):
    sc = _silu(cv_ref[...]).astype(BF16)
    o_ref[...] = _dot(sc, w_ref[...].astype(BF16)) + b_ref[...]


def _modulation(cvecs, ada_w, ada_b):
    depth, _, n_mod = ada_w.shape
    steps = n_mod // MOD_COLS_PER_STEP
    out = pl.pallas_call(
        _mod_kernel,
        grid=(depth, steps),
        in_specs=[
            pl.BlockSpec((MOD_ROWS, D_MODEL), lambda l, n: (0, 0)),
            pl.BlockSpec((None, D_MODEL, MOD_COLS_PER_STEP), lambda l, n: (l, 0, n)),
            pl.BlockSpec((None, 1, MOD_COLS_PER_STEP), lambda l, n: (l, 0, n)),
        ],
        out_specs=pl.BlockSpec((None, MOD_ROWS, MOD_COLS_PER_STEP), lambda l, n: (l, 0, n)),
        out_shape=jax.ShapeDtypeStruct((depth, MOD_ROWS, n_mod), F32),
        compiler_params=pltpu.CompilerParams(
            dimension_semantics=("arbitrary", "arbitrary"), vmem_limit_bytes=VMEM_LIMIT_BYTES),
        name="adaln_modulation",
    )(cvecs, ada_w, ada_b.reshape(depth, 1, n_mod))
    return out.reshape(depth, MOD_ROWS, 6, D_MODEL)


def _inproj_kernel(*refs, n_chunks, use_rope, has_init, rider_steps):
    it = iter(refs)
    x_ref, mod_ref, gain_ref, w_ref = next(it), next(it), next(it), next(it)
    cos_ref = sin_ref = s0_ref = None
    if use_rope:
        cos_ref, sin_ref = next(it), next(it)
    kdf_ref, kdb_ref, cdb_ref = next(it), next(it), next(it)
    if has_init:
        s0_ref = next(it)
    cast_srcs = [next(it) for _ in rider_steps]
    p_ref, dsf_ref, sbp_ref, sb_ref = next(it), next(it), next(it), next(it)
    cast_dsts = [next(it) for _ in rider_steps]

    step = pl.program_id(0) * pl.num_programs(1) + pl.program_id(1)
    _cast_blocks(step, rider_steps, cast_srcs, cast_dsts)

    @pl.when(pl.program_id(1) == 0)
    def _():
        if has_init:
            sb_ref[...] = s0_ref[...]
        else:
            sb_ref[...] = jnp.zeros_like(sb_ref)

    mod = mod_ref[...]
    h = _norm_mod(x_ref[...], gain_ref[...], mod[0:1], mod[1:2]).astype(BF16)
    p = _dot(h, w_ref[...])

    q = p[:, 0:RET_WIDTH] * (HEAD_DIM ** -0.5)
    k = p[:, RET_WIDTH:2 * RET_WIDTH]
    v = p[:, 2 * RET_WIDTH:3 * RET_WIDTH]
    if use_rope:
        cos2, sin2 = cos_ref[...], sin_ref[...]

        def rope(a):
            heads = []
            for hd in range(N_HEADS):
                ah = a[:, hd * HEAD_DIM:(hd + 1) * HEAD_DIM]
                heads.append(ah * cos2 + pltpu.roll(ah, HEAD_DIM // 2, 1) * sin2)
            return jnp.concatenate(heads, axis=1)

        q, k = rope(q), rope(k)

    p_ref[:, P_Q:P_Q + RET_WIDTH] = q.astype(BF16)
    p_ref[:, P_K:P_K + RET_WIDTH] = k.astype(BF16)
    v16 = v.astype(BF16)
    p_ref[:, P_V:P_V + RET_WIDTH] = v16
    p_ref[:, P_G:P_BG + CONV_WIDTH] = p[:, 3 * RET_WIDTH:4 * RET_WIDTH + CONV_WIDTH].astype(BF16)
    cg = p[:, 4 * RET_WIDTH + CONV_WIDTH:4 * RET_WIDTH + 2 * CONV_WIDTH]
    xc = p[:, 4 * RET_WIDTH + 2 * CONV_WIDTH:AB_IN]
    p_ref[:, P_CX:P_COLS] = (cg * xc).astype(BF16)

    tn = (((0,), (0,)), ((), ()))
    for c in reversed(range(n_chunks)):
        rows = slice(c * CHUNK, (c + 1) * CHUNK)
        kc = k[rows]
        kf = (kc * kdf_ref[...]).astype(BF16)
        kb = (kc * kdb_ref[...]).astype(BF16)
        vc = v16[rows]
        for hd in range(N_HEADS):
            cols = slice(hd * HEAD_DIM, (hd + 1) * HEAD_DIM)
            dsf_ref[c, hd] = lax.dot_general(kf[:, cols], vc[:, cols], tn, preferred_element_type=F32)
            dsb = lax.dot_general(kb[:, cols], vc[:, cols], tn, preferred_element_type=F32)
            sb = sb_ref[hd]
            sbp_ref[c, hd] = sb.astype(BF16)
            sb_ref[hd] = sb * cdb_ref[hd] + dsb


def _state_spec(index_of_seq, direction):
    return pl.BlockSpec((None, None, None, N_HEADS, HEAD_DIM, HEAD_DIM),
                        lambda *g: (index_of_seq(*g), 0, direction, 0, 0, 0))


def _inproj(x2d, mod4, w, tables, state, *, n_seq, seq_len, tm, use_rope, mod_row, cast=()):
    nt = seq_len // tm
    n_chunks = tm // CHUNK
    has_init = state is not None
    tile = lambda b, j: b * nt + (nt - 1 - j)
    riders = [_cast_rider(w3d, layer, n_seq * nt, lambda b, j: b * nt + j) for w3d, layer in cast]

    in_specs = [
        pl.BlockSpec((tm, D_MODEL), lambda b, j: (tile(b, j), 0)),
        pl.BlockSpec((None, None, 6, D_MODEL), lambda b, j: (0, mod_row(tile(b, j)), 0, 0)),
        _resident((1, D_MODEL), layer=0),
        _resident((D_MODEL, AB_IN), layer=0),
    ]
    args = [x2d, mod4, w["norm_mix_g"], w["w_in_ab"]]
    if use_rope:
        in_specs += [pl.BlockSpec((tm, HEAD_DIM), lambda b, j: (nt - 1 - j, 0))] * 2
        args += [tables["cos2"], tables["sin2"]]
    in_specs += [
        pl.BlockSpec((CHUNK, RET_WIDTH), lambda b, j: (0, 0)),
        pl.BlockSpec((CHUNK, RET_WIDTH), lambda b, j: (0, 0)),
        pl.BlockSpec((N_HEADS, 1, HEAD_DIM), lambda b, j: (0, 0, 0)),
    ]
    args += [tables["kdec_f"], tables["kdec_b"], tables["cd_b"]]
    if has_init:
        in_specs.append(_state_spec(lambda b, j: b, 1))
        args.append(state)
    in_specs += [r[0] for r in riders]
    args += [w3d for w3d, _ in cast]

    n_tok = n_seq * seq_len
    chunk_spec = pl.BlockSpec((n_chunks, N_HEADS, HEAD_DIM, HEAD_DIM), lambda b, j: (tile(b, j), 0, 0, 0))
    chunk_shape = (n_tok // CHUNK, N_HEADS, HEAD_DIM, HEAD_DIM)
    return pl.pallas_call(
        functools.partial(_inproj_kernel, n_chunks=n_chunks, use_rope=use_rope, has_init=has_init,
                          rider_steps=tuple(r[3] for r in riders)),
        grid=(n_seq, nt),
        in_specs=in_specs,
        out_specs=[
            pl.BlockSpec((tm, P_COLS), lambda b, j: (tile(b, j), 0)),
            chunk_spec,
            chunk_spec,
            pl.BlockSpec((None, N_HEADS, HEAD_DIM, HEAD_DIM), lambda b, j: (b, 0, 0, 0)),
        ] + [r[1] for r in riders],
        out_shape=[
            jax.ShapeDtypeStruct((n_tok, P_COLS), BF16),
            jax.ShapeDtypeStruct(chunk_shape, F32),
            jax.ShapeDtypeStruct(chunk_shape, BF16),
            jax.ShapeDtypeStruct((n_seq, N_HEADS, HEAD_DIM, HEAD_DIM), F32),
        ] + [r[2] for r in riders],
        compiler_params=pltpu.CompilerParams(
            dimension_semantics=("arbitrary", "arbitrary"), vmem_limit_bytes=VMEM_LIMIT_BYTES),
        name="inproj_bwd_scan",
    )(*args)


def _mix0_kernel(*refs, n_chunks, tiles_per_seq, n_tiles, has_init):
    it = iter(refs)
    x_ref, p_ref, hprev_ref, hnext_ref, dsf_ref, sbp_ref, sbfin_ref = (next(it) for _ in range(7))
    s0_ref = next(it) if has_init else None
    (mod_ref, modp_ref, mask_ref, crf_ref, crb_ref, cdf_ref, retg_ref, convw_ref, convb_ref, wout_ref,
     gain_ref, wg_ref, wu_ref, wd_ref) = (next(it) for _ in range(14))
    xo_ref, state_ref = next(it), next(it)
    y_scr, x1_scr, h2_scr, acc_scr = (next(it) for _ in range(4))
    sf_ref = state_ref.at[0]

    i = pl.program_id(0)
    j = lax.rem(i, tiles_per_seq)
    tm = x_ref.shape[0]

    @pl.when(jnp.logical_and(j == 0, i < n_tiles))
    def _():
        state_ref[1] = sbfin_ref[...]
        if has_init:
            sf_ref[...] = s0_ref[...]
        else:
            sf_ref[...] = jnp.zeros_like(sf_ref)

    def mixer_phases():
        nt_dims = (((1,), (1,)), ((), ()))
        live = {}

        def head_cols(base, hd):
            return slice(base + hd * HEAD_DIM, base + (hd + 1) * HEAD_DIM)

        def scores_and_cross(c):
            def phase():
                rows = slice(c * CHUNK, (c + 1) * CHUNK)
                for hd in range(N_HEADS):
                    cols = head_cols(0, hd)
                    qh = p_ref[rows, head_cols(P_Q, hd)]
                    kh = p_ref[rows, head_cols(P_K, hd)]
                    scores = lax.dot_general(qh, kh, nt_dims, preferred_element_type=F32) * mask_ref[hd]
                    sf = sf_ref[hd]
                    states = jnp.concatenate([sf.astype(BF16), sbp_ref[c, hd]], axis=1)
                    qs = _dot(qh, states)
                    cross = qs[:, :HEAD_DIM] * crf_ref[:, cols] + qs[:, HEAD_DIM:] * crb_ref[:, cols]
                    sf_ref[hd] = sf * cdf_ref[hd] + dsf_ref[c, hd]
                    live[c, hd] = (scores.astype(BF16), cross)
            return phase

        def retention_out(c):
            def phase():
                rows = slice(c * CHUNK, (c + 1) * CHUNK)
                for hd in range(N_HEADS):
                    cols = head_cols(0, hd)
                    scores, cross = live.pop((c, hd))
                    o = _dot(scores, p_ref[rows, head_cols(P_V, hd)]) + cross
                    o = o * lax.rsqrt(jnp.mean(o * o, axis=-1, keepdims=True) + EPS) * retg_ref[:, cols]
                    gate = p_ref[rows, head_cols(P_G, hd)].astype(F32)
                    y_scr[rows, cols] = (_silu(gate) * o).astype(BF16)
            return phase

        def conv():
            cx = p_ref[:, P_CX:P_COLS].astype(F32)
            row_id = lax.broadcasted_iota(jnp.int32, cx.shape, 0)
            prev_row = hprev_ref[...].astype(F32)[HALO_ROWS - 1:HALO_ROWS] * jnp.where(j > 0, 1.0, 0.0)
            next_row = hnext_ref[...].astype(F32)[0:1] * jnp.where(j < tiles_per_seq - 1, 1.0, 0.0)
            prev = jnp.where(row_id == 0, prev_row, pltpu.roll(cx, 1, 0))
            nxt = jnp.where(row_id == tm - 1, next_row, pltpu.roll(cx, tm - 1, 0))
            out = convw_ref[0:1] * prev + convw_ref[1:2] * cx + convw_ref[2:3] * nxt + convb_ref[...]
            y_scr[:, RET_WIDTH:] = (p_ref[:, P_BG:P_BG + CONV_WIDTH].astype(F32) * out).astype(BF16)

        def out_proj():
            live["x1"] = x_ref[...] + mod_ref[2:3] * _dot(y_scr[...], wout_ref[...])

        def norm_ffn():
            live["h2"] = _norm_mod(live["x1"], gain_ref[...], mod_ref[3:4], mod_ref[4:5]).astype(BF16)

        def stage():
            x1_scr[...] = live.pop("x1")
            h2_scr[...] = live.pop("h2")

        chunks = range(n_chunks)
        retention = _pipelined([scores_and_cross(c) for c in chunks], [retention_out(c) for c in chunks])
        return retention + [conv, out_proj, norm_ffn], stage

    def finish_ffn():
        xo_ref[...] = x1_scr[...] + modp_ref[5:6] * acc_scr[...]

    @pl.when(i == 0)
    def _():
        phases, stage = mixer_phases()
        _run(phases + [stage])

    @pl.when(jnp.logical_and(i > 0, i < n_tiles))
    def _():
        phases, stage = mixer_phases()
        ffn = _ffn_phases(h2_scr, acc_scr, wg_ref, wu_ref, wd_ref)
        _run(_weave(phases, ffn, [0] + [1] * (len(phases) - 1)) + [finish_ffn, stage])

    @pl.when(i == n_tiles)
    def _():
        _run(_ffn_phases(h2_scr, acc_scr, wg_ref, wu_ref, wd_ref) + [finish_ffn])


def _mix0(x2d, p, dsf, sbp, sb_fin, mod4, tables, state, w, *, n_seq, seq_len, tm, mod_row):
    tiles_per_seq = seq_len // tm
    n_chunks = tm // CHUNK
    has_init = state is not None
    n_tok = n_seq * seq_len
    n_tiles = n_tok // tm
    halo_per_tile = tm // HALO_ROWS
    last_halo = n_tok // HALO_ROWS - 1
    cx_block = P_CX // CONV_WIDTH
    cur = lambda i: jnp.minimum(i, n_tiles - 1)
    prev = lambda i: jnp.maximum(i - 1, 0)
    seq = lambda i: cur(i) // tiles_per_seq

    chunk_spec = pl.BlockSpec((n_chunks, N_HEADS, HEAD_DIM, HEAD_DIM), lambda i: (cur(i), 0, 0, 0))
    in_specs = [
        pl.BlockSpec((tm, D_MODEL), lambda i: (cur(i), 0)),
        pl.BlockSpec((tm, P_COLS), lambda i: (cur(i), 0)),
        pl.BlockSpec((HALO_ROWS, CONV_WIDTH),
                     lambda i: (jnp.maximum(cur(i) * halo_per_tile - 1, 0), cx_block)),
        pl.BlockSpec((HALO_ROWS, CONV_WIDTH),
                     lambda i: (jnp.minimum((cur(i) + 1) * halo_per_tile, last_halo), cx_block)),
        chunk_spec,
        chunk_spec,
        pl.BlockSpec((None, N_HEADS, HEAD_DIM, HEAD_DIM), lambda i: (seq(i), 0, 0, 0)),
    ]
    args = [x2d, p, p, p, dsf, sbp, sb_fin]
    if has_init:
        in_specs.append(_state_spec(seq, 0))
        args.append(state)
    in_specs += [
        pl.BlockSpec((None, None, 6, D_MODEL), lambda i: (0, mod_row(cur(i)), 0, 0)),
        pl.BlockSpec((None, None, 6, D_MODEL), lambda i: (0, mod_row(prev(i)), 0, 0)),
        _resident((N_HEADS, CHUNK, CHUNK)),
        _resident((CHUNK, RET_WIDTH)),
        _resident((CHUNK, RET_WIDTH)),
        _resident((N_HEADS, 1, HEAD_DIM)),
        _resident((1, RET_WIDTH), layer=0),
        _resident((3, CONV_WIDTH), layer=0),
        _resident((1, CONV_WIDTH), layer=0),
        _resident((D_MODEL, D_MODEL)),
        _resident((1, D_MODEL), layer=0),
        _resident((D_MODEL, FFN_HIDDEN)),
        _resident((D_MODEL, FFN_HIDDEN)),
        _resident((FFN_HIDDEN, D_MODEL)),
    ]
    args += [mod4, mod4, tables["mask"], tables["cross_f"], tables["cross_b"], tables["cd_f"],
             w["ret_norm_g"], w["conv_w"], w["conv_b"], w["w_out_ab"], w["norm_ffn_g"],
             w["w_gate0"], w["w_up0"], w["w_down0"]]
    return pl.pallas_call(
        functools.partial(_mix0_kernel, n_chunks=n_chunks, tiles_per_seq=tiles_per_seq, n_tiles=n_tiles,
                          has_init=has_init),
        grid=(n_tiles + 1,),
        in_specs=in_specs,
        out_specs=[
            pl.BlockSpec((tm, D_MODEL), lambda i: (prev(i), 0)),
            pl.BlockSpec((None, 2, N_HEADS, HEAD_DIM, HEAD_DIM), lambda i: (seq(i), 0, 0, 0, 0)),
        ],
        out_shape=[
            jax.ShapeDtypeStruct((n_tok, D_MODEL), F32),
            jax.ShapeDtypeStruct((n_seq, 2, N_HEADS, HEAD_DIM, HEAD_DIM), F32),
        ],
        scratch_shapes=[pltpu.VMEM((tm, D_MODEL), BF16), pltpu.VMEM((tm, D_MODEL), F32),
                        pltpu.VMEM((tm, D_MODEL), BF16), pltpu.VMEM((tm, D_MODEL), F32)],
        compiler_params=pltpu.CompilerParams(
            dimension_semantics=("arbitrary",), vmem_limit_bytes=VMEM_LIMIT_BYTES),
        name="retention_conv_ffn0",
    )(*args)


def _mix1_kernel(x_ref, mod_ref, modp_ref, gain_ref, win_ref, vg_ref, ws_ref, bs_ref, wout_ref, gain2_ref,
                 wg_ref, wu_ref, wd_ref, gfin_ref, o_ref, gated_scr, x1_scr, h2_scr, acc_scr, *,
                 n_chunks, n_tiles):
    i = pl.program_id(0)

    def mixer_phases():
        live = {}

        def norm_mix():
            live["h"] = _norm_mod(x_ref[...], gain_ref[...], mod_ref[0:1], mod_ref[1:2]).astype(BF16)

        def proj_u():
            live["u"] = _gelu_tanh(_dot(live["h"], win_ref[:, :D_MODEL]))

        def proj_v():
            live["v"] = _gelu_tanh(_dot(live.pop("h"), win_ref[:, D_MODEL:]))

        def norm_v():
            v = live.pop("v")
            live["v"] = (v * lax.rsqrt(jnp.mean(v * v, axis=-1, keepdims=True) + EPS) * vg_ref[...]).astype(BF16)

        def spatial(c):
            def phase():
                rows = slice(c * CHUNK, (c + 1) * CHUNK)
                for g in range(CMLP_GROUPS):
                    cols = slice(g * GROUP_WIDTH, (g + 1) * GROUP_WIDTH)
                    s = _dot(ws_ref[g], live["v"][rows, cols]) + bs_ref[g]
                    gated_scr[rows, cols] = (live["u"][rows, cols] * s).astype(BF16)
            return phase

        def out_proj():
            live["x1"] = x_ref[...] + mod_ref[2:3] * _dot(gated_scr[...], wout_ref[...])

        def norm_ffn():
            live["h2"] = _norm_mod(live["x1"], gain2_ref[...], mod_ref[3:4], mod_ref[4:5]).astype(BF16)

        def stage():
            x1_scr[...] = live.pop("x1")
            h2_scr[...] = live.pop("h2")

        phases = [norm_mix, proj_u, proj_v, norm_v] + [spatial(c) for c in range(n_chunks)]
        return phases + [out_proj, norm_ffn], stage

    def finish_ffn():
        x2 = x1_scr[...] + modp_ref[5:6] * acc_scr[...]
        o_ref[...] = x2 * lax.rsqrt(jnp.mean(x2 * x2, axis=-1, keepdims=True) + EPS) * gfin_ref[...]

    @pl.when(i == 0)
    def _():
        phases, stage = mixer_phases()
        _run(phases + [stage])

    @pl.when(jnp.logical_and(i > 0, i < n_tiles))
    def _():
        phases, stage = mixer_phases()
        ffn = _ffn_phases(h2_scr, acc_scr, wg_ref, wu_ref, wd_ref)
        gaps = [1, 2, 2, 4, 2] + [0] * (n_chunks - 1) + [2, 2]
        _run(_weave(phases, ffn, gaps) + [finish_ffn, stage])

    @pl.when(i == n_tiles)
    def _():
        _run(_ffn_phases(h2_scr, acc_scr, wg_ref, wu_ref, wd_ref) + [finish_ffn])


def _mix1(x2d, mod4, w, *, tm, mod_row):
    n_tok = x2d.shape[0]
    n_chunks = tm // CHUNK
    n_tiles = n_tok // tm
    cur = lambda i: jnp.minimum(i, n_tiles - 1)
    prev = lambda i: jnp.maximum(i - 1, 0)
    in_specs = [
        pl.BlockSpec((tm, D_MODEL), lambda i: (cur(i), 0)),
        pl.BlockSpec((None, None, 6, D_MODEL), lambda i: (1, mod_row(cur(i)), 0, 0)),
        pl.BlockSpec((None, None, 6, D_MODEL), lambda i: (1, mod_row(prev(i)), 0, 0)),
        _resident((1, D_MODEL), layer=1),
        _resident((D_MODEL, 2 * D_MODEL)),
        _resident((1, D_MODEL), layer=0),
        _resident((CMLP_GROUPS, CHUNK, CHUNK), layer=0),
        _resident((CMLP_GROUPS, CHUNK, 1), layer=0),
        _resident((D_MODEL, D_MODEL)),
        _resident((1, D_MODEL), layer=1),
        _resident((D_MODEL, FFN_HIDDEN)),
        _resident((D_MODEL, FFN_HIDDEN)),
        _resident((FFN_HIDDEN, D_MODEL)),
        _resident((1, D_MODEL)),
    ]
    args = [x2d, mod4, mod4, w["norm_mix_g"], w["w_in_c"], w["c_norm_g"], w["w_spatial"], w["b_spatial"],
            w["w_out_c"], w["norm_ffn_g"], w["w_gate1"], w["w_up1"], w["w_down1"], w["final_norm_g"]]
    return pl.pallas_call(
        functools.partial(_mix1_kernel, n_chunks=n_chunks, n_tiles=n_tiles),
        grid=(n_tiles + 1,),
        in_specs=in_specs,
        out_specs=pl.BlockSpec((tm, D_MODEL), lambda i: (prev(i), 0)),
        out_shape=jax.ShapeDtypeStruct((n_tok, D_MODEL), F32),
        scratch_shapes=[pltpu.VMEM((tm, D_MODEL), BF16), pltpu.VMEM((tm, D_MODEL), F32),
                        pltpu.VMEM((tm, D_MODEL), BF16), pltpu.VMEM((tm, D_MODEL), F32)],
        compiler_params=pltpu.CompilerParams(
            dimension_semantics=("arbitrary",), vmem_limit_bytes=VMEM_LIMIT_BYTES),
        name="chunkmlp_ffn1",
    )(*args)


def _decay_tables(decay_logit):
    lg = jax.nn.log_sigmoid(decay_logit.astype(F32))
    idx = jnp.arange(CHUNK, dtype=F32)
    diff = idx[:, None] - idx[None, :]
    lg_f, lg_b = lg[0], lg[1]
    mask_f = jnp.where(diff >= 0, jnp.exp(jnp.maximum(diff, 0.0) * lg_f[:, None, None]), 0.0)
    mask_b = jnp.where(diff <= 0, jnp.exp(jnp.maximum(-diff, 0.0) * lg_b[:, None, None]), 0.0)
    per_head = lambda t: jnp.repeat(t.T, HEAD_DIM, axis=1)
    lanes = lambda t: jnp.broadcast_to(t[:, None, None], (N_HEADS, 1, HEAD_DIM))
    return {
        "mask": mask_f + mask_b,
        "cross_f": per_head(jnp.exp((idx + 1.0) * lg_f[:, None])),
        "cross_b": per_head(jnp.exp((CHUNK - idx) * lg_b[:, None])),
        "kdec_f": per_head(jnp.exp((CHUNK - 1.0 - idx) * lg_f[:, None])),
        "kdec_b": per_head(jnp.exp(idx * lg_b[:, None])),
        "cd_f": lanes(jnp.exp(CHUNK * lg_f)),
        "cd_b": lanes(jnp.exp(CHUNK * lg_b)),
    }


def _rope_tables(seq_len):
    rows = seq_len // GRID_W
    row = np.repeat(np.arange(rows, dtype=np.float32), GRID_W)
    col = np.tile(np.arange(GRID_W, dtype=np.float32), rows)
    nf = HEAD_DIM // 4
    freqs = np.float32(ROPE_BASE) ** (-np.arange(nf, dtype=np.float32) / np.float32(nf))
    ang = np.concatenate([row[:, None] * freqs, col[:, None] * freqs], axis=-1).astype(np.float64)
    cos, sin = np.cos(ang).astype(np.float32), np.sin(ang).astype(np.float32)
    return {"cos2": jnp.asarray(np.concatenate([cos, cos], axis=-1)),
            "sin2": jnp.asarray(np.concatenate([-sin, sin], axis=-1))}


def _pass_geometry(x, use_rope, mod_row_of_seq):
    n_seq, seq_len, _ = x.shape
    tm = min(TOKEN_TILE, seq_len)
    tiles_per_seq = seq_len // tm
    return dict(n_seq=n_seq, seq_len=seq_len, tm=tm, use_rope=use_rope,
                mod_row=lambda t: mod_row_of_seq(t // tiles_per_seq))


def kernel(x_prompt, x_sample, state_ret, c, c_ctx, ada_w, ada_b, norm_mix_g, norm_ffn_g, w_in_ab,
           ret_decay_logit, ret_norm_g, conv_w, conv_b, w_out_ab, w_in_c, c_norm_g, w_spatial, b_spatial,
           w_out_c, w_gate, w_up, w_down, final_norm_g):
    n_lat = c.shape[0]
    cvecs = jnp.concatenate(
        [c_ctx[None, :], c, jnp.zeros((MOD_ROWS - 1 - n_lat, D_MODEL), F32)], axis=0)
    mod4 = _modulation(cvecs, ada_w, ada_b)

    rows = lambda a: a[..., None, :]
    w = {
        "norm_mix_g": rows(norm_mix_g), "norm_ffn_g": rows(norm_ffn_g),
        "w_in_ab": w_in_ab.astype(BF16), "ret_norm_g": rows(ret_norm_g),
        "conv_w": conv_w, "conv_b": rows(conv_b), "c_norm_g": rows(c_norm_g),
        "w_spatial": w_spatial.astype(BF16), "b_spatial": b_spatial[..., None],
        "final_norm_g": final_norm_g.reshape(1, -1),
    }
    tables = _decay_tables(ret_decay_logit[0])
    lat_tables = dict(tables, **_rope_tables(x_sample.shape[1]))
    ctx = _pass_geometry(x_prompt, False, lambda b: 0)
    lat = _pass_geometry(x_sample, True, lambda b: b + 1)
    flat = lambda x: x.reshape(-1, D_MODEL)
    without_rope = lambda g: {k: v for k, v in g.items() if k != "use_rope"}

    later_weights = {"w_out_ab": (w_out_ab, 0), "w_gate0": (w_gate, 0), "w_up0": (w_up, 0),
                     "w_down0": (w_down, 0), "w_in_c": (w_in_c, 0), "w_out_c": (w_out_c, 0),
                     "w_gate1": (w_gate, 1), "w_up1": (w_up, 1), "w_down1": (w_down, 1)}
    p_lat, dsf_lat, sbp_lat, sbfin_lat, *converted = _inproj(
        flat(x_sample), mod4, w, lat_tables, state_ret, **lat, cast=tuple(later_weights.values()))
    w.update(zip(later_weights, converted))
    p_ctx, dsf_ctx, sbp_ctx, sbfin_ctx = _inproj(flat(x_prompt), mod4, w, tables, None, **ctx)

    x1_lat, _ = _mix0(flat(x_sample), p_lat, dsf_lat, sbp_lat, sbfin_lat, mod4, lat_tables, state_ret, w,
                      **without_rope(lat))
    x1_ctx, state_ctx = _mix0(flat(x_prompt), p_ctx, dsf_ctx, sbp_ctx, sbfin_ctx, mod4, tables, None, w,
                              **without_rope(ctx))

    y_sample = _mix1(x1_lat, mod4, w, tm=lat["tm"], mod_row=lat["mod_row"])
    y_prompt = _mix1(x1_ctx, mod4, w, tm=ctx["tm"], mod_row=ctx["mod_row"])
    return (y_prompt.reshape(x_prompt.shape), y_sample.reshape(x_sample.shape), state_ctx[:, None])
```

```python
import functools
import math

import numpy as np
import jax
import jax.numpy as jnp
from jax import lax
from jax.experimental import pallas as pl
from jax.experimental.pallas import tpu as pltpu

D_MODEL = 1024
N_HEADS = 4
HEAD_DIM = 128
RET_WIDTH = N_HEADS * HEAD_DIM
CONV_WIDTH = 512
CHUNK = 128
CMLP_GROUPS = 4
GROUP_WIDTH = D_MODEL // CMLP_GROUPS
FFN_HIDDEN = 2816
AB_IN = 4 * RET_WIDTH + 3 * CONV_WIDTH
GRID_W = 64
ROPE_BASE = 10000.0
EPS = 1e-6
MOD_ROWS = 16
MOD_COLS_PER_STEP = 1536

P_Q, P_K, P_V, P_G, P_BG, P_CX = 0, 512, 1024, 1536, 2048, 2560
P_COLS = 3072
BF16_SUBLANES = 16
HALO_ROWS = BF16_SUBLANES

MXU_TILE = 256
TOKEN_TILE = 512
FFN_SLABS = tuple((lo, lo + MXU_TILE) for lo in range(0, FFN_HIDDEN, MXU_TILE))
VMEM_LIMIT_BYTES = 56 * 1024 * 1024
F32 = jnp.float32
BF16 = jnp.bfloat16


def _dot(a, b):
    return jnp.dot(a, b, preferred_element_type=F32)


def _resident(shape, layer=None):
    if layer is None:
        zeros = (0,) * len(shape)
        return pl.BlockSpec(shape, lambda *_: zeros, pipeline_mode=pl.Buffered(1))
    index = (layer,) + (0,) * len(shape)
    return pl.BlockSpec((None,) + tuple(shape), lambda *_: index, pipeline_mode=pl.Buffered(1))


def _norm_mod(x, gain, shift, scale):
    y = x * lax.rsqrt(jnp.mean(x * x, axis=-1, keepdims=True) + EPS)
    return (y * gain) * (1.0 + scale) + shift


def _silu(x):
    hx = 0.5 * x
    return hx + hx * jnp.tanh(hx)


def _gelu_tanh(x):
    return 0.5 * x * (1.0 + jnp.tanh(math.sqrt(2.0 / math.pi) * (x + 0.044715 * (x * x * x))))


def _pipelined(first_halves, second_halves):
    order = [first_halves[0]]
    for k in range(1, len(first_halves)):
        order += [first_halves[k], second_halves[k - 1]]
    order.append(second_halves[-1])
    return order


def _weave(primary, filler, gaps):
    filler = list(filler)
    order = []
    for phase, gap in zip(primary, gaps):
        order += filler[:gap] + [phase]
        filler = filler[gap:]
    return order + filler


def _run(phases):
    for phase in phases:
        phase()


def _cast_rider(w3d, layer, max_steps, step_of):
    _, rows, cols = w3d.shape
    n_steps = max(n for n in range(1, max_steps + 1)
                  if rows % n == 0 and (rows // n) % BF16_SUBLANES == 0)
    block_rows = rows // n_steps
    blk = lambda *g: jnp.minimum(step_of(*g), n_steps - 1)
    return (pl.BlockSpec((None, block_rows, cols), lambda *g: (layer, blk(*g), 0)),
            pl.BlockSpec((block_rows, cols), lambda *g: (blk(*g), 0)),
            jax.ShapeDtypeStruct((rows, cols), BF16), n_steps)


def _cast_blocks(step, rider_steps, srcs, dsts):
    for n_steps, src, dst in zip(rider_steps, srcs, dsts):
        @pl.when(step < n_steps)
        def _(src=src, dst=dst):
            dst[...] = src[...].astype(BF16)


def _ffn_phases(h2_scr, x1_scr, x1_hold_scr, gate2_ref, dst_ref, wg_ref, wu_ref, wd_ref):
    acts = {}
    last = len(FFN_SLABS) - 1

    def up(k):
        def phase():
            lo, hi = FFN_SLABS[k]
            h2 = h2_scr[...]
            acts[k] = (_silu(_dot(h2, wg_ref[:, lo:hi])) * _dot(h2, wu_ref[:, lo:hi])).astype(BF16)
        return phase

    def down(k):
        def phase():
            lo, hi = FFN_SLABS[k]
            part = _dot(acts.pop(k), wd_ref[lo:hi, :])
            if k == 0:
                x1_hold_scr[...] = x1_scr[...]
                dst_ref[...] = part
            elif k < last:
                dst_ref[...] += part
            else:
                dst_ref[...] = x1_hold_scr[...] + gate2_ref[5:6] * (dst_ref[...] + part)
        return phase

    slabs = range(len(FFN_SLABS))
    return _pipelined([up(k) for k in slabs], [down(k) for k in slabs])


def _mod_kernel(cv_ref, w_ref, b_ref, o_ref):
    sc = _silu(cv_ref[...]).astype(BF16)
    o_ref[...] = _dot(sc, w_ref[...].astype(BF16)) + b_ref[...]


def _modulation(cvecs, ada_w, ada_b):
    depth, _, n_mod = ada_w.shape
    steps = n_mod // MOD_COLS_PER_STEP
    out = pl.pallas_call(
        _mod_kernel,
        grid=(depth, steps),
        in_specs=[
            pl.BlockSpec((MOD_ROWS, D_MODEL), lambda l, n: (0, 0)),
            pl.BlockSpec((None, D_MODEL, MOD_COLS_PER_STEP), lambda l, n: (l, 0, n)),
            pl.BlockSpec((None, 1, MOD_COLS_PER_STEP), lambda l, n: (l, 0, n)),
        ],
        out_specs=pl.BlockSpec((None, MOD_ROWS, MOD_COLS_PER_STEP), lambda l, n: (l, 0, n)),
        out_shape=jax.ShapeDtypeStruct((depth, MOD_ROWS, n_mod), F32),
        compiler_params=pltpu.CompilerParams(
            dimension_semantics=("arbitrary", "arbitrary"), vmem_limit_bytes=VMEM_LIMIT_BYTES),
        name="adaln_modulation",
    )(cvecs, ada_w, ada_b.reshape(depth, 1, n_mod))
    return out.reshape(depth, MOD_ROWS, 6, D_MODEL)


def _inproj_kernel(*refs, n_chunks, use_rope, has_init, rider_steps):
    it = iter(refs)
    x_ref, mod_ref, gain_ref, w_ref = next(it), next(it), next(it), next(it)
    cos_ref = sin_ref = s0_ref = None
    if use_rope:
        cos_ref, sin_ref = next(it), next(it)
    kdf_ref, kdb_ref, cdb_ref = next(it), next(it), next(it)
    if has_init:
        s0_ref = next(it)
    cast_srcs = [next(it) for _ in rider_steps]
    p_ref, dsf_ref, sbp_ref, sb_ref = next(it), next(it), next(it), next(it)
    cast_dsts = [next(it) for _ in rider_steps]

    step = pl.program_id(0) * pl.num_programs(1) + pl.program_id(1)
    _cast_blocks(step, rider_steps, cast_srcs, cast_dsts)

    @pl.when(pl.program_id(1) == 0)
    def _():
        if has_init:
            sb_ref[...] = s0_ref[...]
        else:
            sb_ref[...] = jnp.zeros_like(sb_ref)

    mod = mod_ref[...]
    h = _norm_mod(x_ref[...], gain_ref[...], mod[0:1], mod[1:2]).astype(BF16)
    p = _dot(h, w_ref[...])

    q = p[:, 0:RET_WIDTH] * (HEAD_DIM ** -0.5)
    k = p[:, RET_WIDTH:2 * RET_WIDTH]
    v = p[:, 2 * RET_WIDTH:3 * RET_WIDTH]
    if use_rope:
        cos2, sin2 = cos_ref[...], sin_ref[...]

        def rope(a):
            heads = []
            for hd in range(N_HEADS):
                ah = a[:, hd * HEAD_DIM:(hd + 1) * HEAD_DIM]
                heads.append(ah * cos2 + pltpu.roll(ah, HEAD_DIM // 2, 1) * sin2)
            return jnp.concatenate(heads, axis=1)

        q, k = rope(q), rope(k)

    p_ref[:, P_Q:P_Q + RET_WIDTH] = q.astype(BF16)
    p_ref[:, P_K:P_K + RET_WIDTH] = k.astype(BF16)
    v16 = v.astype(BF16)
    p_ref[:, P_V:P_V + RET_WIDTH] = v16
    p_ref[:, P_G:P_BG + CONV_WIDTH] = p[:, 3 * RET_WIDTH:4 * RET_WIDTH + CONV_WIDTH].astype(BF16)
    cg = p[:, 4 * RET_WIDTH + CONV_WIDTH:4 * RET_WIDTH + 2 * CONV_WIDTH]
    xc = p[:, 4 * RET_WIDTH + 2 * CONV_WIDTH:AB_IN]
    p_ref[:, P_CX:P_COLS] = (cg * xc).astype(BF16)

    tn = (((0,), (0,)), ((), ()))
    for c in reversed(range(n_chunks)):
        rows = slice(c * CHUNK, (c + 1) * CHUNK)
        kc = k[rows]
        kf = (kc * kdf_ref[...]).astype(BF16)
        kb = (kc * kdb_ref[...]).astype(BF16)
        vc = v16[rows]
        for hd in range(N_HEADS):
            cols = slice(hd * HEAD_DIM, (hd + 1) * HEAD_DIM)
            dsf_ref[c, hd] = lax.dot_general(kf[:, cols], vc[:, cols], tn, preferred_element_type=F32)
            dsb = lax.dot_general(kb[:, cols], vc[:, cols], tn, preferred_element_type=F32)
            sb = sb_ref[hd]
            sbp_ref[c, hd] = sb.astype(BF16)
            sb_ref[hd] = sb * cdb_ref[hd] + dsb


def _state_spec(index_of_seq, direction):
    return pl.BlockSpec((None, None, None, N_HEADS, HEAD_DIM, HEAD_DIM),
                        lambda *g: (index_of_seq(*g), 0, direction, 0, 0, 0))


def _inproj(x2d, mod4, w, tables, state, *, n_seq, seq_len, tm, use_rope, mod_row, cast=()):
    nt = seq_len // tm
    n_chunks = tm // CHUNK
    has_init = state is not None
    tile = lambda b, j: b * nt + (nt - 1 - j)
    riders = [_cast_rider(w3d, layer, n_seq * nt, lambda b, j: b * nt + j) for w3d, layer in cast]

    in_specs = [
        pl.BlockSpec((tm, D_MODEL), lambda b, j: (tile(b, j), 0)),
        pl.BlockSpec((None, None, 6, D_MODEL), lambda b, j: (0, mod_row(tile(b, j)), 0, 0)),
        _resident((1, D_MODEL), layer=0),
        _resident((D_MODEL, AB_IN), layer=0),
    ]
    args = [x2d, mod4, w["norm_mix_g"], w["w_in_ab"]]
    if use_rope:
        in_specs += [pl.BlockSpec((tm, HEAD_DIM), lambda b, j: (nt - 1 - j, 0))] * 2
        args += [tables["cos2"], tables["sin2"]]
    in_specs += [
        pl.BlockSpec((CHUNK, RET_WIDTH), lambda b, j: (0, 0)),
        pl.BlockSpec((CHUNK, RET_WIDTH), lambda b, j: (0, 0)),
        pl.BlockSpec((N_HEADS, 1, HEAD_DIM), lambda b, j: (0, 0, 0)),
    ]
    args += [tables["kdec_f"], tables["kdec_b"], tables["cd_b"]]
    if has_init:
        in_specs.append(_state_spec(lambda b, j: b, 1))
        args.append(state)
    in_specs += [r[0] for r in riders]
    args += [w3d for w3d, _ in cast]

    n_tok = n_seq * seq_len
    chunk_spec = pl.BlockSpec((n_chunks, N_HEADS, HEAD_DIM, HEAD_DIM), lambda b, j: (tile(b, j), 0, 0, 0))
    chunk_shape = (n_tok // CHUNK, N_HEADS, HEAD_DIM, HEAD_DIM)
    return pl.pallas_call(
        functools.partial(_inproj_kernel, n_chunks=n_chunks, use_rope=use_rope, has_init=has_init,
                          rider_steps=tuple(r[3] for r in riders)),
        grid=(n_seq, nt),
        in_specs=in_specs,
        out_specs=[
            pl.BlockSpec((tm, P_COLS), lambda b, j: (tile(b, j), 0)),
            chunk_spec,
            chunk_spec,
            pl.BlockSpec((None, N_HEADS, HEAD_DIM, HEAD_DIM), lambda b, j: (b, 0, 0, 0)),
        ] + [r[1] for r in riders],
        out_shape=[
            jax.ShapeDtypeStruct((n_tok, P_COLS), BF16),
            jax.ShapeDtypeStruct(chunk_shape, F32),
            jax.ShapeDtypeStruct(chunk_shape, BF16),
            jax.ShapeDtypeStruct((n_seq, N_HEADS, HEAD_DIM, HEAD_DIM), F32),
        ] + [r[2] for r in riders],
        compiler_params=pltpu.CompilerParams(
            dimension_semantics=("arbitrary", "arbitrary"), vmem_limit_bytes=VMEM_LIMIT_BYTES),
        name="inproj_bwd_scan",
    )(*args)


def _mix0_kernel(*refs, n_chunks, tiles_per_seq, n_tiles, has_init):
    it = iter(refs)
    x_ref, p_ref, hprev_ref, hnext_ref, dsf_ref, sbp_ref, sbfin_ref = (next(it) for _ in range(7))
    s0_ref = next(it) if has_init else None
    (mod_ref, modp_ref, mask_ref, crf_ref, crb_ref, cdf_ref, retg_ref, convw_ref, convb_ref, wout_ref,
     gain_ref, wg_ref, wu_ref, wd_ref) = (next(it) for _ in range(14))
    xo_ref, state_ref = next(it), next(it)
    y_scr, x1_scr, h2_scr, x1_hold_scr = (next(it) for _ in range(4))
    sf_ref = state_ref.at[0]

    i = pl.program_id(0)
    j = lax.rem(i, tiles_per_seq)
    tm = x_ref.shape[0]

    @pl.when(jnp.logical_and(j == 0, i < n_tiles))
    def _():
        state_ref[1] = sbfin_ref[...]
        if has_init:
            sf_ref[...] = s0_ref[...]
        else:
            sf_ref[...] = jnp.zeros_like(sf_ref)

    def mixer_phases():
        nt_dims = (((1,), (1,)), ((), ()))
        live = {}

        def head_cols(base, hd):
            return slice(base + hd * HEAD_DIM, base + (hd + 1) * HEAD_DIM)

        def scores_and_cross(c):
            def phase():
                rows = slice(c * CHUNK, (c + 1) * CHUNK)
                for hd in range(N_HEADS):
                    cols = head_cols(0, hd)
                    qh = p_ref[rows, head_cols(P_Q, hd)]
                    kh = p_ref[rows, head_cols(P_K, hd)]
                    scores = lax.dot_general(qh, kh, nt_dims, preferred_element_type=F32) * mask_ref[hd]
                    sf = sf_ref[hd]
                    states = jnp.concatenate([sf.astype(BF16), sbp_ref[c, hd]], axis=1)
                    qs = _dot(qh, states)
                    cross = qs[:, :HEAD_DIM] * crf_ref[:, cols] + qs[:, HEAD_DIM:] * crb_ref[:, cols]
                    sf_ref[hd] = sf * cdf_ref[hd] + dsf_ref[c, hd]
                    live[c, hd] = (scores.astype(BF16), cross)
            return phase

        def retention_out(c):
            def phase():
                rows = slice(c * CHUNK, (c + 1) * CHUNK)
                for hd in range(N_HEADS):
                    cols = head_cols(0, hd)
                    scores, cross = live.pop((c, hd))
                    o = _dot(scores, p_ref[rows, head_cols(P_V, hd)]) + cross
                    o = o * lax.rsqrt(jnp.mean(o * o, axis=-1, keepdims=True) + EPS) * retg_ref[:, cols]
                    gate = p_ref[rows, head_cols(P_G, hd)].astype(F32)
                    y_scr[rows, cols] = (_silu(gate) * o).astype(BF16)
            return phase

        def conv():
            cx = p_ref[:, P_CX:P_COLS].astype(F32)
            row_id = lax.broadcasted_iota(jnp.int32, cx.shape, 0)
            prev_row = hprev_ref[...].astype(F32)[HALO_ROWS - 1:HALO_ROWS] * jnp.where(j > 0, 1.0, 0.0)
            next_row = hnext_ref[...].astype(F32)[0:1] * jnp.where(j < tiles_per_seq - 1, 1.0, 0.0)
            prev = jnp.where(row_id == 0, prev_row, pltpu.roll(cx, 1, 0))
            nxt = jnp.where(row_id == tm - 1, next_row, pltpu.roll(cx, tm - 1, 0))
            out = convw_ref[0:1] * prev + convw_ref[1:2] * cx + convw_ref[2:3] * nxt + convb_ref[...]
            y_scr[:, RET_WIDTH:] = (p_ref[:, P_BG:P_BG + CONV_WIDTH].astype(F32) * out).astype(BF16)

        def out_proj():
            x1 = x_ref[...] + mod_ref[2:3] * _dot(y_scr[...], wout_ref[...])
            x1_scr[...] = x1
            live["h2"] = _norm_mod(x1, gain_ref[...], mod_ref[3:4], mod_ref[4:5]).astype(BF16)

        def stage_h2():
            h2_scr[...] = live.pop("h2")

        chunks = range(n_chunks)
        retention = _pipelined([scores_and_cross(c) for c in chunks], [retention_out(c) for c in chunks])
        return retention + [conv, out_proj], stage_h2

    def ffn_phases():
        return _ffn_phases(h2_scr, x1_scr, x1_hold_scr, modp_ref, xo_ref, wg_ref, wu_ref, wd_ref)

    @pl.when(i == 0)
    def _():
        phases, stage_h2 = mixer_phases()
        _run(phases + [stage_h2])

    @pl.when(jnp.logical_and(i > 0, i < n_tiles))
    def _():
        phases, stage_h2 = mixer_phases()
        ffn = ffn_phases()
        _run(_weave(phases, ffn[:-2], [0] + [1] * (len(phases) - 1)) + [stage_h2] + ffn[-2:])

    @pl.when(i == n_tiles)
    def _():
        _run(ffn_phases())


def _mix0(x2d, p, dsf, sbp, sb_fin, mod4, tables, state, w, *, n_seq, seq_len, tm, mod_row):
    tiles_per_seq = seq_len // tm
    n_chunks = tm // CHUNK
    has_init = state is not None
    n_tok = n_seq * seq_len
    n_tiles = n_tok // tm
    halo_per_tile = tm // HALO_ROWS
    last_halo = n_tok // HALO_ROWS - 1
    cx_block = P_CX // CONV_WIDTH
    cur = lambda i: jnp.minimum(i, n_tiles - 1)
    prev = lambda i: jnp.maximum(i - 1, 0)
    seq = lambda i: cur(i) // tiles_per_seq

    chunk_spec = pl.BlockSpec((n_chunks, N_HEADS, HEAD_DIM, HEAD_DIM), lambda i: (cur(i), 0, 0, 0))
    in_specs = [
        pl.BlockSpec((tm, D_MODEL), lambda i: (cur(i), 0)),
        pl.BlockSpec((tm, P_COLS), lambda i: (cur(i), 0)),
        pl.BlockSpec((HALO_ROWS, CONV_WIDTH),
                     lambda i: (jnp.maximum(cur(i) * halo_per_tile - 1, 0), cx_block)),
        pl.BlockSpec((HALO_ROWS, CONV_WIDTH),
                     lambda i: (jnp.minimum((cur(i) + 1) * halo_per_tile, last_halo), cx_block)),
        chunk_spec,
        chunk_spec,
        pl.BlockSpec((None, N_HEADS, HEAD_DIM, HEAD_DIM), lambda i: (seq(i), 0, 0, 0)),
    ]
    args = [x2d, p, p, p, dsf, sbp, sb_fin]
    if has_init:
        in_specs.append(_state_spec(seq, 0))
        args.append(state)
    in_specs += [
        pl.BlockSpec((None, None, 6, D_MODEL), lambda i: (0, mod_row(cur(i)), 0, 0)),
        pl.BlockSpec((None, None, 6, D_MODEL), lambda i: (0, mod_row(prev(i)), 0, 0)),
        _resident((N_HEADS, CHUNK, CHUNK)),
        _resident((CHUNK, RET_WIDTH)),
        _resident((CHUNK, RET_WIDTH)),
        _resident((N_HEADS, 1, HEAD_DIM)),
        _resident((1, RET_WIDTH), layer=0),
        _resident((3, CONV_WIDTH), layer=0),
        _resident((1, CONV_WIDTH), layer=0),
        _resident((D_MODEL, D_MODEL)),
        _resident((1, D_MODEL), layer=0),
        _resident((D_MODEL, FFN_HIDDEN)),
        _resident((D_MODEL, FFN_HIDDEN)),
        _resident((FFN_HIDDEN, D_MODEL)),
    ]
    args += [mod4, mod4, tables["mask"], tables["cross_f"], tables["cross_b"], tables["cd_f"],
             w["ret_norm_g"], w["conv_w"], w["conv_b"], w["w_out_ab"], w["norm_ffn_g"],
             w["w_gate0"], w["w_up0"], w["w_down0"]]
    return pl.pallas_call(
        functools.partial(_mix0_kernel, n_chunks=n_chunks, tiles_per_seq=tiles_per_seq, n_tiles=n_tiles,
                          has_init=has_init),
        grid=(n_tiles + 1,),
        in_specs=in_specs,
        out_specs=[
            pl.BlockSpec((tm, D_MODEL), lambda i: (prev(i), 0)),
            pl.BlockSpec((None, 2, N_HEADS, HEAD_DIM, HEAD_DIM), lambda i: (seq(i), 0, 0, 0, 0)),
        ],
        out_shape=[
            jax.ShapeDtypeStruct((n_tok, D_MODEL), F32),
            jax.ShapeDtypeStruct((n_seq, 2, N_HEADS, HEAD_DIM, HEAD_DIM), F32),
        ],
        scratch_shapes=[pltpu.VMEM((tm, D_MODEL), BF16), pltpu.VMEM((tm, D_MODEL), F32),
                        pltpu.VMEM((tm, D_MODEL), BF16), pltpu.VMEM((tm, D_MODEL), F32)],
        compiler_params=pltpu.CompilerParams(
            dimension_semantics=("arbitrary",), vmem_limit_bytes=VMEM_LIMIT_BYTES),
        name="retention_conv_ffn0",
    )(*args)


def _mix1_kernel(x_ref, mod_ref, modp_ref, gain_ref, win_ref, vg_ref, ws_ref, bs_ref, wout_ref, gain2_ref,
                 wg_ref, wu_ref, wd_ref, gfin_ref, o_ref, gated_scr, x1_scr, h2_scr, x2_scr, x1_hold_scr, *,
                 n_chunks, n_tiles):
    i = pl.program_id(0)

    def mixer_phases():
        live = {}

        def norm_mix():
            live["h"] = _norm_mod(x_ref[...], gain_ref[...], mod_ref[0:1], mod_ref[1:2]).astype(BF16)

        def proj_u():
            live["u"] = _gelu_tanh(_dot(live["h"], win_ref[:, :D_MODEL]))

        def proj_v():
            live["v"] = _gelu_tanh(_dot(live.pop("h"), win_ref[:, D_MODEL:]))

        def norm_v():
            v = live.pop("v")
            live["v"] = (v * lax.rsqrt(jnp.mean(v * v, axis=-1, keepdims=True) + EPS) * vg_ref[...]).astype(BF16)

        def spatial(c):
            def phase():
                rows = slice(c * CHUNK, (c + 1) * CHUNK)
                for g in range(CMLP_GROUPS):
                    cols = slice(g * GROUP_WIDTH, (g + 1) * GROUP_WIDTH)
                    s = _dot(ws_ref[g], live["v"][rows, cols]) + bs_ref[g]
                    gated_scr[rows, cols] = (live["u"][rows, cols] * s).astype(BF16)
            return phase

        def out_proj():
            x1 = x_ref[...] + mod_ref[2:3] * _dot(gated_scr[...], wout_ref[...])
            x1_scr[...] = x1
            live["h2"] = _norm_mod(x1, gain2_ref[...], mod_ref[3:4], mod_ref[4:5]).astype(BF16)

        def stage_h2():
            h2_scr[...] = live.pop("h2")

        phases = [norm_mix, proj_u, proj_v, norm_v] + [spatial(c) for c in range(n_chunks)]
        return phases + [out_proj], stage_h2

    def ffn_phases():
        return _ffn_phases(h2_scr, x1_scr, x1_hold_scr, modp_ref, x2_scr, wg_ref, wu_ref, wd_ref)

    def final_norm():
        x2 = x2_scr[...]
        o_ref[...] = x2 * lax.rsqrt(jnp.mean(x2 * x2, axis=-1, keepdims=True) + EPS) * gfin_ref[...]

    @pl.when(i == 0)
    def _():
        phases, stage_h2 = mixer_phases()
        x2_scr[...] = jnp.zeros_like(x2_scr)
        _run(phases + [stage_h2])

    @pl.when(jnp.logical_and(i > 0, i < n_tiles))
    def _():
        phases, stage_h2 = mixer_phases()
        ffn = ffn_phases()
        gaps = [3, 2, 3, 4, 3] + [0] * (n_chunks - 1) + [3]
        _run([final_norm] + _weave(phases, ffn[:-2], gaps) + [stage_h2] + ffn[-2:])

    @pl.when(i == n_tiles)
    def _():
        _run([final_norm] + ffn_phases())

    @pl.when(i == n_tiles + 1)
    def _():
        final_norm()


def _mix1(x2d, mod4, w, *, tm, mod_row):
    n_tok = x2d.shape[0]
    n_chunks = tm // CHUNK
    n_tiles = n_tok // tm
    tile_of = lambda i, lag: jnp.clip(i - lag, 0, n_tiles - 1)
    in_specs = [
        pl.BlockSpec((tm, D_MODEL), lambda i: (tile_of(i, 0), 0)),
        pl.BlockSpec((None, None, 6, D_MODEL), lambda i: (1, mod_row(tile_of(i, 0)), 0, 0)),
        pl.BlockSpec((None, None, 6, D_MODEL), lambda i: (1, mod_row(tile_of(i, 1)), 0, 0)),
        _resident((1, D_MODEL), layer=1),
        _resident((D_MODEL, 2 * D_MODEL)),
        _resident((1, D_MODEL), layer=0),
        _resident((CMLP_GROUPS, CHUNK, CHUNK), layer=0),
        _resident((CMLP_GROUPS, CHUNK, 1), layer=0),
        _resident((D_MODEL, D_MODEL)),
        _resident((1, D_MODEL), layer=1),
        _resident((D_MODEL, FFN_HIDDEN)),
        _resident((D_MODEL, FFN_HIDDEN)),
        _resident((FFN_HIDDEN, D_MODEL)),
        _resident((1, D_MODEL)),
    ]
    args = [x2d, mod4, mod4, w["norm_mix_g"], w["w_in_c"], w["c_norm_g"], w["w_spatial"], w["b_spatial"],
            w["w_out_c"], w["norm_ffn_g"], w["w_gate1"], w["w_up1"], w["w_down1"], w["final_norm_g"]]
    return pl.pallas_call(
        functools.partial(_mix1_kernel, n_chunks=n_chunks, n_tiles=n_tiles),
        grid=(n_tiles + 2,),
        in_specs=in_specs,
        out_specs=pl.BlockSpec((tm, D_MODEL), lambda i: (tile_of(i, 2), 0)),
        out_shape=jax.ShapeDtypeStruct((n_tok, D_MODEL), F32),
        scratch_shapes=[pltpu.VMEM((tm, D_MODEL), BF16), pltpu.VMEM((tm, D_MODEL), F32),
                        pltpu.VMEM((tm, D_MODEL), BF16), pltpu.VMEM((tm, D_MODEL), F32),
                        pltpu.VMEM((tm, D_MODEL), F32)],
        compiler_params=pltpu.CompilerParams(
            dimension_semantics=("arbitrary",), vmem_limit_bytes=VMEM_LIMIT_BYTES),
        name="chunkmlp_ffn1",
    )(*args)


def _decay_tables(decay_logit):
    lg = jax.nn.log_sigmoid(decay_logit.astype(F32))
    idx = jnp.arange(CHUNK, dtype=F32)
    diff = idx[:, None] - idx[None, :]
    lg_f, lg_b = lg[0], lg[1]
    mask_f = jnp.where(diff >= 0, jnp.exp(jnp.maximum(diff, 0.0) * lg_f[:, None, None]), 0.0)
    mask_b = jnp.where(diff <= 0, jnp.exp(jnp.maximum(-diff, 0.0) * lg_b[:, None, None]), 0.0)
    per_head = lambda t: jnp.repeat(t.T, HEAD_DIM, axis=1)
    lanes = lambda t: jnp.broadcast_to(t[:, None, None], (N_HEADS, 1, HEAD_DIM))
    return {
        "mask": mask_f + mask_b,
        "cross_f": per_head(jnp.exp((idx + 1.0) * lg_f[:, None])),
        "cross_b": per_head(jnp.exp((CHUNK - idx) * lg_b[:, None])),
        "kdec_f": per_head(jnp.exp((CHUNK - 1.0 - idx) * lg_f[:, None])),
        "kdec_b": per_head(jnp.exp(idx * lg_b[:, None])),
        "cd_f": lanes(jnp.exp(CHUNK * lg_f)),
        "cd_b": lanes(jnp.exp(CHUNK * lg_b)),
    }


def _rope_tables(seq_len):
    rows = seq_len // GRID_W
    row = np.repeat(np.arange(rows, dtype=np.float32), GRID_W)
    col = np.tile(np.arange(GRID_W, dtype=np.float32), rows)
    nf = HEAD_DIM // 4
    freqs = np.float32(ROPE_BASE) ** (-np.arange(nf, dtype=np.float32) / np.float32(nf))
    ang = np.concatenate([row[:, None] * freqs, col[:, None] * freqs], axis=-1).astype(np.float64)
    cos, sin = np.cos(ang).astype(np.float32), np.sin(ang).astype(np.float32)
    return {"cos2": jnp.asarray(np.concatenate([cos, cos], axis=-1)),
            "sin2": jnp.asarray(np.concatenate([-sin, sin], axis=-1))}


def _pass_geometry(x, use_rope, mod_row_of_seq):
    n_seq, seq_len, _ = x.shape
    tm = min(TOKEN_TILE, seq_len)
    tiles_per_seq = seq_len // tm
    return dict(n_seq=n_seq, seq_len=seq_len, tm=tm, use_rope=use_rope,
                mod_row=lambda t: mod_row_of_seq(t // tiles_per_seq))


def kernel(x_prompt, x_sample, state_ret, c, c_ctx, ada_w, ada_b, norm_mix_g, norm_ffn_g, w_in_ab,
           ret_decay_logit, ret_norm_g, conv_w, conv_b, w_out_ab, w_in_c, c_norm_g, w_spatial, b_spatial,
           w_out_c, w_gate, w_up, w_down, final_norm_g):
    n_lat = c.shape[0]
    cvecs = jnp.concatenate(
        [c_ctx[None, :], c, jnp.zeros((MOD_ROWS - 1 - n_lat, D_MODEL), F32)], axis=0)
    mod4 = _modulation(cvecs, ada_w, ada_b)

    rows = lambda a: a[..., None, :]
    w = {
        "norm_mix_g": rows(norm_mix_g), "norm_ffn_g": rows(norm_ffn_g),
        "w_in_ab": w_in_ab.astype(BF16), "ret_norm_g": rows(ret_norm_g),
        "conv_w": conv_w, "conv_b": rows(conv_b), "c_norm_g": rows(c_norm_g),
        "w_spatial": w_spatial.astype(BF16), "b_spatial": b_spatial[..., None],
        "final_norm_g": final_norm_g.reshape(1, -1),
    }
    tables = _decay_tables(ret_decay_logit[0])
    lat_tables = dict(tables, **_rope_tables(x_sample.shape[1]))
    ctx = _pass_geometry(x_prompt, False, lambda b: 0)
    lat = _pass_geometry(x_sample, True, lambda b: b + 1)
    flat = lambda x: x.reshape(-1, D_MODEL)
    without_rope = lambda g: {k: v for k, v in g.items() if k != "use_rope"}

    later_weights = {"w_out_ab": (w_out_ab, 0), "w_gate0": (w_gate, 0), "w_up0": (w_up, 0),
                     "w_down0": (w_down, 0), "w_in_c": (w_in_c, 0), "w_out_c": (w_out_c, 0),
                     "w_gate1": (w_gate, 1), "w_up1": (w_up, 1), "w_down1": (w_down, 1)}
    p_lat, dsf_lat, sbp_lat, sbfin_lat, *converted = _inproj(
        flat(x_sample), mod4, w, lat_tables, state_ret, **lat, cast=tuple(later_weights.values()))
    w.update(zip(later_weights, converted))
    p_ctx, dsf_ctx, sbp_ctx, sbfin_ctx = _inproj(flat(x_prompt), mod4, w, tables, None, **ctx)

    x1_lat, _ = _mix0(flat(x_sample), p_lat, dsf_lat, sbp_lat, sbfin_lat, mod4, lat_tables, state_ret, w,
                      **without_rope(lat))
    x1_ctx, state_ctx = _mix0(flat(x_prompt), p_ctx, dsf_ctx, sbp_ctx, sbfin_ctx, mod4, tables, None, w,
                              **without_rope(ctx))

    y_sample = _mix1(x1_lat, mod4, w, tm=lat["tm"], mod_row=lat["mod_row"])
    y_prompt = _mix1(x1_ctx, mod4, w, tm=ctx["tm"], mod_row=ctx["mod_row"])
    return (y_prompt.reshape(x_prompt.shape), y_sample.reshape(x_sample.shape), state_ctx[:, None])
```

```python
import functools
import math

import numpy as np
import jax
import jax.numpy as jnp
from jax import lax
from jax.experimental import pallas as pl
from jax.experimental.pallas import tpu as pltpu

D_MODEL = 1024
N_HEADS = 4
HEAD_DIM = 128
RET_WIDTH = N_HEADS * HEAD_DIM
CONV_WIDTH = 512
CHUNK = 128
CMLP_GROUPS = 4
GROUP_WIDTH = D_MODEL // CMLP_GROUPS
FFN_HIDDEN = 2816
AB_IN = 4 * RET_WIDTH + 3 * CONV_WIDTH
GRID_W = 64
ROPE_BASE = 10000.0
EPS = 1e-6
MOD_ROWS = 16
MOD_COLS_PER_STEP = 1536

P_Q, P_K, P_V, P_G, P_BG, P_CX = 0, 512, 1024, 1536, 2048, 2560
P_COLS = 3072
LANES = 128
BF16_SUBLANES = 16
HALO_ROWS = BF16_SUBLANES

MXU_TILE = 256
TOKEN_TILE = 512
FFN_SLABS = tuple((lo, lo + MXU_TILE) for lo in range(0, FFN_HIDDEN, MXU_TILE))
VMEM_LIMIT_BYTES = 56 * 1024 * 1024
F32 = jnp.float32
BF16 = jnp.bfloat16


def _dot(a, b):
    return jnp.dot(a, b, preferred_element_type=F32)


def _resident(shape, layer=None):
    if layer is None:
        zeros = (0,) * len(shape)
        return pl.BlockSpec(shape, lambda *_: zeros, pipeline_mode=pl.Buffered(1))
    index = (layer,) + (0,) * len(shape)
    return pl.BlockSpec((None,) + tuple(shape), lambda *_: index, pipeline_mode=pl.Buffered(1))


def _norm_mod(x, gain, shift, scale):
    y = x * lax.rsqrt(jnp.mean(x * x, axis=-1, keepdims=True) + EPS)
    return (y * gain) * (1.0 + scale) + shift


def _silu(x):
    hx = 0.5 * x
    return hx + hx * jnp.tanh(hx)


def _gelu_tanh(x):
    return 0.5 * x * (1.0 + jnp.tanh(math.sqrt(2.0 / math.pi) * (x + 0.044715 * (x * x * x))))


def _pipelined(first_halves, second_halves):
    order = [first_halves[0]]
    for k in range(1, len(first_halves)):
        order += [first_halves[k], second_halves[k - 1]]
    order.append(second_halves[-1])
    return order


def _weave(primary, filler, gaps):
    filler = list(filler)
    order = []
    for phase, gap in zip(primary, gaps):
        order += filler[:gap] + [phase]
        filler = filler[gap:]
    return order + filler


def _run(phases):
    for phase in phases:
        phase()


def _cast_rider(w3d, layer, n_steps, step_of):
    _, rows, cols = w3d.shape
    for n_col in range(1, cols // LANES + 1):
        n_row, rem = divmod(n_steps, n_col)
        if (not rem and cols % (n_col * LANES) == 0 and rows % n_row == 0
                and (rows // n_row) % BF16_SUBLANES == 0):
            break
    else:
        raise ValueError(f"a {rows} x {cols} weight does not split into {n_steps} bf16-tileable blocks")
    block = (rows // n_row, cols // n_col)
    index = lambda *g: (step_of(*g) // n_col, step_of(*g) % n_col)
    return (pl.BlockSpec((None,) + block, lambda *g: (layer,) + index(*g)),
            pl.BlockSpec(block, index),
            jax.ShapeDtypeStruct((rows, cols), BF16))


def _ffn_phases(h2_scr, x1_scr, x1_hold_scr, gate2_ref, dst_ref, wg_ref, wu_ref, wd_ref):
    acts = {}
    last = len(FFN_SLABS) - 1

    def up(k):
        def phase():
            lo, hi = FFN_SLABS[k]
            h2 = h2_scr[...]
            acts[k] = (_silu(_dot(h2, wg_ref[:, lo:hi])) * _dot(h2, wu_ref[:, lo:hi])).astype(BF16)
        return phase

    def down(k):
        def phase():
            lo, hi = FFN_SLABS[k]
            part = _dot(acts.pop(k), wd_ref[lo:hi, :])
            if k == 0:
                x1_hold_scr[...] = x1_scr[...]
                dst_ref[...] = part
            elif k < last:
                dst_ref[...] += part
            else:
                dst_ref[...] = x1_hold_scr[...] + gate2_ref[5:6] * (dst_ref[...] + part)
        return phase

    slabs = range(len(FFN_SLABS))
    return _pipelined([up(k) for k in slabs], [down(k) for k in slabs])


def _mod_kernel(cv_ref, w_ref, b_ref, o_ref):
    sc = _silu(cv_ref[...]).astype(BF16)
    o_ref[...] = _dot(sc, w_ref[...].astype(BF16)) + b_ref[...]


def _modulation(cvecs, ada_w, ada_b):
    depth, _, n_mod = ada_w.shape
    steps = n_mod // MOD_COLS_PER_STEP
    out = pl.pallas_call(
        _mod_kernel,
        grid=(depth, steps),
        in_specs=[
            pl.BlockSpec((MOD_ROWS, D_MODEL), lambda l, n: (0, 0)),
            pl.BlockSpec((None, D_MODEL, MOD_COLS_PER_STEP), lambda l, n: (l, 0, n)),
            pl.BlockSpec((None, 1, MOD_COLS_PER_STEP), lambda l, n: (l, 0, n)),
        ],
        out_specs=pl.BlockSpec((None, MOD_ROWS, MOD_COLS_PER_STEP), lambda l, n: (l, 0, n)),
        out_shape=jax.ShapeDtypeStruct((depth, MOD_ROWS, n_mod), F32),
        compiler_params=pltpu.CompilerParams(
            dimension_semantics=("arbitrary", "arbitrary"), vmem_limit_bytes=VMEM_LIMIT_BYTES),
        name="adaln_modulation",
    )(cvecs, ada_w, ada_b.reshape(depth, 1, n_mod))
    return out.reshape(depth, MOD_ROWS, 6, D_MODEL)


def _inproj_kernel(*refs, n_chunks, use_rope, has_init, n_riders):
    it = iter(refs)
    x_ref, mod_ref, gain_ref, w_ref = next(it), next(it), next(it), next(it)
    cos_ref = sin_ref = s0_ref = None
    if use_rope:
        cos_ref, sin_ref = next(it), next(it)
    kdf_ref, kdb_ref, cdb_ref = next(it), next(it), next(it)
    if has_init:
        s0_ref = next(it)
    cast_srcs = [next(it) for _ in range(n_riders)]
    p_ref, dsf_ref, sbp_ref, sb_ref = next(it), next(it), next(it), next(it)
    cast_dsts = [next(it) for _ in range(n_riders)]

    @pl.when(pl.program_id(1) == 0)
    def _():
        if has_init:
            sb_ref[...] = s0_ref[...]
        else:
            sb_ref[...] = jnp.zeros_like(sb_ref)

    for src, dst in zip(cast_srcs, cast_dsts):
        dst[...] = src[...].astype(BF16)

    mod = mod_ref[...]
    h = _norm_mod(x_ref[...], gain_ref[...], mod[0:1], mod[1:2]).astype(BF16)
    p = _dot(h, w_ref[...])

    q = p[:, 0:RET_WIDTH] * (HEAD_DIM ** -0.5)
    k = p[:, RET_WIDTH:2 * RET_WIDTH]
    v = p[:, 2 * RET_WIDTH:3 * RET_WIDTH]
    if use_rope:
        cos2, sin2 = cos_ref[...], sin_ref[...]

        def rope(a):
            heads = []
            for hd in range(N_HEADS):
                ah = a[:, hd * HEAD_DIM:(hd + 1) * HEAD_DIM]
                heads.append(ah * cos2 + pltpu.roll(ah, HEAD_DIM // 2, 1) * sin2)
            return jnp.concatenate(heads, axis=1)

        q, k = rope(q), rope(k)

    p_ref[:, P_Q:P_Q + RET_WIDTH] = q.astype(BF16)
    p_ref[:, P_K:P_K + RET_WIDTH] = k.astype(BF16)
    v16 = v.astype(BF16)
    p_ref[:, P_V:P_V + RET_WIDTH] = v16
    p_ref[:, P_G:P_BG + CONV_WIDTH] = p[:, 3 * RET_WIDTH:4 * RET_WIDTH + CONV_WIDTH].astype(BF16)
    cg = p[:, 4 * RET_WIDTH + CONV_WIDTH:4 * RET_WIDTH + 2 * CONV_WIDTH]
    xc = p[:, 4 * RET_WIDTH + 2 * CONV_WIDTH:AB_IN]
    p_ref[:, P_CX:P_COLS] = (cg * xc).astype(BF16)

    tn = (((0,), (0,)), ((), ()))
    for c in reversed(range(n_chunks)):
        rows = slice(c * CHUNK, (c + 1) * CHUNK)
        kc = k[rows]
        kf = (kc * kdf_ref[...]).astype(BF16)
        kb = (kc * kdb_ref[...]).astype(BF16)
        vc = v16[rows]
        for hd in range(N_HEADS):
            cols = slice(hd * HEAD_DIM, (hd + 1) * HEAD_DIM)
            dsf_ref[c, hd] = lax.dot_general(kf[:, cols], vc[:, cols], tn, preferred_element_type=F32)
            dsb = lax.dot_general(kb[:, cols], vc[:, cols], tn, preferred_element_type=F32)
            sb = sb_ref[hd]
            sbp_ref[c, hd] = sb.astype(BF16)
            sb_ref[hd] = sb * cdb_ref[hd] + dsb


def _state_spec(index_of_seq, direction):
    return pl.BlockSpec((None, None, None, N_HEADS, HEAD_DIM, HEAD_DIM),
                        lambda *g: (index_of_seq(*g), 0, direction, 0, 0, 0))


def _inproj(x2d, mod4, w, tables, state, *, n_seq, seq_len, tm, use_rope, mod_row, cast=()):
    nt = seq_len // tm
    n_chunks = tm // CHUNK
    has_init = state is not None
    tile = lambda b, j: b * nt + (nt - 1 - j)
    riders = [_cast_rider(w3d, layer, n_seq * nt, lambda b, j: b * nt + j) for w3d, layer in cast]

    in_specs = [
        pl.BlockSpec((tm, D_MODEL), lambda b, j: (tile(b, j), 0)),
        pl.BlockSpec((None, None, 6, D_MODEL), lambda b, j: (0, mod_row(tile(b, j)), 0, 0)),
        _resident((1, D_MODEL), layer=0),
        _resident((D_MODEL, AB_IN), layer=0),
    ]
    args = [x2d, mod4, w["norm_mix_g"], w["w_in_ab"]]
    if use_rope:
        in_specs += [pl.BlockSpec((tm, HEAD_DIM), lambda b, j: (nt - 1 - j, 0))] * 2
        args += [tables["cos2"], tables["sin2"]]
    in_specs += [
        pl.BlockSpec((CHUNK, RET_WIDTH), lambda b, j: (0, 0)),
        pl.BlockSpec((CHUNK, RET_WIDTH), lambda b, j: (0, 0)),
        pl.BlockSpec((N_HEADS, 1, HEAD_DIM), lambda b, j: (0, 0, 0)),
    ]
    args += [tables["kdec_f"], tables["kdec_b"], tables["cd_b"]]
    if has_init:
        in_specs.append(_state_spec(lambda b, j: b, 1))
        args.append(state)
    in_specs += [r[0] for r in riders]
    args += [w3d for w3d, _ in cast]

    n_tok = n_seq * seq_len
    chunk_spec = pl.BlockSpec((n_chunks, N_HEADS, HEAD_DIM, HEAD_DIM), lambda b, j: (tile(b, j), 0, 0, 0))
    chunk_shape = (n_tok // CHUNK, N_HEADS, HEAD_DIM, HEAD_DIM)
    return pl.pallas_call(
        functools.partial(_inproj_kernel, n_chunks=n_chunks, use_rope=use_rope, has_init=has_init,
                          n_riders=len(riders)),
        grid=(n_seq, nt),
        in_specs=in_specs,
        out_specs=[
            pl.BlockSpec((tm, P_COLS), lambda b, j: (tile(b, j), 0)),
            chunk_spec,
            chunk_spec,
            pl.BlockSpec((None, N_HEADS, HEAD_DIM, HEAD_DIM), lambda b, j: (b, 0, 0, 0)),
        ] + [r[1] for r in riders],
        out_shape=[
            jax.ShapeDtypeStruct((n_tok, P_COLS), BF16),
            jax.ShapeDtypeStruct(chunk_shape, F32),
            jax.ShapeDtypeStruct(chunk_shape, BF16),
            jax.ShapeDtypeStruct((n_seq, N_HEADS, HEAD_DIM, HEAD_DIM), F32),
        ] + [r[2] for r in riders],
        compiler_params=pltpu.CompilerParams(
            dimension_semantics=("arbitrary", "arbitrary"), vmem_limit_bytes=VMEM_LIMIT_BYTES),
        name="inproj_bwd_scan",
    )(*args)


def _mix0_kernel(*refs, n_chunks, tiles_per_seq, n_tiles, has_init):
    it = iter(refs)
    x_ref, p_ref, hprev_ref, hnext_ref, dsf_ref, sbp_ref, sbfin_ref = (next(it) for _ in range(7))
    s0_ref = next(it) if has_init else None
    (mod_ref, modp_ref, mask_ref, crf_ref, crb_ref, cdf_ref, retg_ref, convw_ref, convb_ref, wout_ref,
     gain_ref, wg_ref, wu_ref, wd_ref) = (next(it) for _ in range(14))
    xo_ref, state_ref = next(it), next(it)
    y_scr, x1_scr, h2_scr, x1_hold_scr = (next(it) for _ in range(4))
    sf_ref = state_ref.at[0]

    i = pl.program_id(0)
    j = lax.rem(i, tiles_per_seq)
    tm = x_ref.shape[0]

    @pl.when(jnp.logical_and(j == 0, i < n_tiles))
    def _():
        state_ref[1] = sbfin_ref[...]
        if has_init:
            sf_ref[...] = s0_ref[...]
        else:
            sf_ref[...] = jnp.zeros_like(sf_ref)

    def mixer_phases():
        nt_dims = (((1,), (1,)), ((), ()))
        live = {}

        def head_cols(base, hd):
            return slice(base + hd * HEAD_DIM, base + (hd + 1) * HEAD_DIM)

        def scores_and_cross(c):
            def phase():
                rows = slice(c * CHUNK, (c + 1) * CHUNK)
                for hd in range(N_HEADS):
                    cols = head_cols(0, hd)
                    qh = p_ref[rows, head_cols(P_Q, hd)]
                    kh = p_ref[rows, head_cols(P_K, hd)]
                    scores = lax.dot_general(qh, kh, nt_dims, preferred_element_type=F32) * mask_ref[hd]
                    sf = sf_ref[hd]
                    states = jnp.concatenate([sf.astype(BF16), sbp_ref[c, hd]], axis=1)
                    qs = _dot(qh, states)
                    cross = qs[:, :HEAD_DIM] * crf_ref[:, cols] + qs[:, HEAD_DIM:] * crb_ref[:, cols]
                    sf_ref[hd] = sf * cdf_ref[hd] + dsf_ref[c, hd]
                    live[c, hd] = (scores.astype(BF16), cross)
            return phase

        def retention_out(c):
            def phase():
                rows = slice(c * CHUNK, (c + 1) * CHUNK)
                for hd in range(N_HEADS):
                    cols = head_cols(0, hd)
                    scores, cross = live.pop((c, hd))
                    o = _dot(scores, p_ref[rows, head_cols(P_V, hd)]) + cross
                    o = o * lax.rsqrt(jnp.mean(o * o, axis=-1, keepdims=True) + EPS) * retg_ref[:, cols]
                    gate = p_ref[rows, head_cols(P_G, hd)].astype(F32)
                    y_scr[rows, cols] = (_silu(gate) * o).astype(BF16)
            return phase

        def conv():
            cx = p_ref[:, P_CX:P_COLS].astype(F32)
            row_id = lax.broadcasted_iota(jnp.int32, cx.shape, 0)
            prev_row = hprev_ref[...].astype(F32)[HALO_ROWS - 1:HALO_ROWS] * jnp.where(j > 0, 1.0, 0.0)
            next_row = hnext_ref[...].astype(F32)[0:1] * jnp.where(j < tiles_per_seq - 1, 1.0, 0.0)
            prev = jnp.where(row_id == 0, prev_row, pltpu.roll(cx, 1, 0))
            nxt = jnp.where(row_id == tm - 1, next_row, pltpu.roll(cx, tm - 1, 0))
            out = convw_ref[0:1] * prev + convw_ref[1:2] * cx + convw_ref[2:3] * nxt + convb_ref[...]
            y_scr[:, RET_WIDTH:] = (p_ref[:, P_BG:P_BG + CONV_WIDTH].astype(F32) * out).astype(BF16)

        def out_proj():
            x1 = x_ref[...] + mod_ref[2:3] * _dot(y_scr[...], wout_ref[...])
            x1_scr[...] = x1
            live["h2"] = _norm_mod(x1, gain_ref[...], mod_ref[3:4], mod_ref[4:5]).astype(BF16)

        def stage_h2():
            h2_scr[...] = live.pop("h2")

        chunks = range(n_chunks)
        retention = _pipelined([scores_and_cross(c) for c in chunks], [retention_out(c) for c in chunks])
        return retention + [conv, out_proj], stage_h2

    def ffn_phases():
        return _ffn_phases(h2_scr, x1_scr, x1_hold_scr, modp_ref, xo_ref, wg_ref, wu_ref, wd_ref)

    @pl.when(i == 0)
    def _():
        phases, stage_h2 = mixer_phases()
        _run(phases + [stage_h2])

    @pl.when(jnp.logical_and(i > 0, i < n_tiles))
    def _():
        phases, stage_h2 = mixer_phases()
        ffn = ffn_phases()
        _run(_weave(phases, ffn[:-2], [0] + [1] * (len(phases) - 1)) + [stage_h2] + ffn[-2:])

    @pl.when(i == n_tiles)
    def _():
        _run(ffn_phases())


def _mix0(x2d, p, dsf, sbp, sb_fin, mod4, tables, state, w, *, n_seq, seq_len, tm, mod_row):
    tiles_per_seq = seq_len // tm
    n_chunks = tm // CHUNK
    has_init = state is not None
    n_tok = n_seq * seq_len
    n_tiles = n_tok // tm
    halo_per_tile = tm // HALO_ROWS
    last_halo = n_tok // HALO_ROWS - 1
    cx_block = P_CX // CONV_WIDTH
    cur = lambda i: jnp.minimum(i, n_tiles - 1)
    prev = lambda i: jnp.maximum(i - 1, 0)
    seq = lambda i: cur(i) // tiles_per_seq

    chunk_spec = pl.BlockSpec((n_chunks, N_HEADS, HEAD_DIM, HEAD_DIM), lambda i: (cur(i), 0, 0, 0))
    in_specs = [
        pl.BlockSpec((tm, D_MODEL), lambda i: (cur(i), 0)),
        pl.BlockSpec((tm, P_COLS), lambda i: (cur(i), 0)),
        pl.BlockSpec((HALO_ROWS, CONV_WIDTH),
                     lambda i: (jnp.maximum(cur(i) * halo_per_tile - 1, 0), cx_block)),
        pl.BlockSpec((HALO_ROWS, CONV_WIDTH),
                     lambda i: (jnp.minimum((cur(i) + 1) * halo_per_tile, last_halo), cx_block)),
        chunk_spec,
        chunk_spec,
        pl.BlockSpec((None, N_HEADS, HEAD_DIM, HEAD_DIM), lambda i: (seq(i), 0, 0, 0)),
    ]
    args = [x2d, p, p, p, dsf, sbp, sb_fin]
    if has_init:
        in_specs.append(_state_spec(seq, 0))
        args.append(state)
    in_specs += [
        pl.BlockSpec((None, None, 6, D_MODEL), lambda i: (0, mod_row(cur(i)), 0, 0)),
        pl.BlockSpec((None, None, 6, D_MODEL), lambda i: (0, mod_row(prev(i)), 0, 0)),
        _resident((N_HEADS, CHUNK, CHUNK)),
        _resident((CHUNK, RET_WIDTH)),
        _resident((CHUNK, RET_WIDTH)),
        _resident((N_HEADS, 1, HEAD_DIM)),
        _resident((1, RET_WIDTH), layer=0),
        _resident((3, CONV_WIDTH), layer=0),
        _resident((1, CONV_WIDTH), layer=0),
        _resident((D_MODEL, D_MODEL)),
        _resident((1, D_MODEL), layer=0),
        _resident((D_MODEL, FFN_HIDDEN)),
        _resident((D_MODEL, FFN_HIDDEN)),
        _resident((FFN_HIDDEN, D_MODEL)),
    ]
    args += [mod4, mod4, tables["mask"], tables["cross_f"], tables["cross_b"], tables["cd_f"],
             w["ret_norm_g"], w["conv_w"], w["conv_b"], w["w_out_ab"], w["norm_ffn_g"],
             w["w_gate0"], w["w_up0"], w["w_down0"]]
    return pl.pallas_call(
        functools.partial(_mix0_kernel, n_chunks=n_chunks, tiles_per_seq=tiles_per_seq, n_tiles=n_tiles,
                          has_init=has_init),
        grid=(n_tiles + 1,),
        in_specs=in_specs,
        out_specs=[
            pl.BlockSpec((tm, D_MODEL), lambda i: (prev(i), 0)),
            pl.BlockSpec((None, 2, N_HEADS, HEAD_DIM, HEAD_DIM), lambda i: (seq(i), 0, 0, 0, 0)),
        ],
        out_shape=[
            jax.ShapeDtypeStruct((n_tok, D_MODEL), F32),
            jax.ShapeDtypeStruct((n_seq, 2, N_HEADS, HEAD_DIM, HEAD_DIM), F32),
        ],
        scratch_shapes=[pltpu.VMEM((tm, D_MODEL), BF16), pltpu.VMEM((tm, D_MODEL), F32),
                        pltpu.VMEM((tm, D_MODEL), BF16), pltpu.VMEM((tm, D_MODEL), F32)],
        compiler_params=pltpu.CompilerParams(
            dimension_semantics=("arbitrary",), vmem_limit_bytes=VMEM_LIMIT_BYTES),
        name="retention_conv_ffn0",
    )(*args)


def _mix1_kernel(x_ref, mod_ref, modp_ref, gain_ref, win_ref, vg_ref, ws_ref, bs_ref, wout_ref, gain2_ref,
                 wg_ref, wu_ref, wd_ref, gfin_ref, o_ref, gated_scr, x1_scr, h2_scr, x2_scr, x1_hold_scr, *,
                 n_chunks, n_tiles):
    i = pl.program_id(0)

    def mixer_phases():
        live = {}

        def norm_mix():
            live["h"] = _norm_mod(x_ref[...], gain_ref[...], mod_ref[0:1], mod_ref[1:2]).astype(BF16)

        def proj_u():
            live["u"] = _gelu_tanh(_dot(live["h"], win_ref[:, :D_MODEL]))

        def proj_v():
            live["v"] = _gelu_tanh(_dot(live.pop("h"), win_ref[:, D_MODEL:]))

        def norm_v():
            v = live.pop("v")
            live["v"] = (v * lax.rsqrt(jnp.mean(v * v, axis=-1, keepdims=True) + EPS) * vg_ref[...]).astype(BF16)

        def spatial(c):
            def phase():
                rows = slice(c * CHUNK, (c + 1) * CHUNK)
                for g in range(CMLP_GROUPS):
                    cols = slice(g * GROUP_WIDTH, (g + 1) * GROUP_WIDTH)
                    s = _dot(ws_ref[g], live["v"][rows, cols]) + bs_ref[g]
                    gated_scr[rows, cols] = (live["u"][rows, cols] * s).astype(BF16)
            return phase

        def out_proj():
            x1 = x_ref[...] + mod_ref[2:3] * _dot(gated_scr[...], wout_ref[...])
            x1_scr[...] = x1
            live["h2"] = _norm_mod(x1, gain2_ref[...], mod_ref[3:4], mod_ref[4:5]).astype(BF16)

        def stage_h2():
            h2_scr[...] = live.pop("h2")

        phases = [norm_mix, proj_u, proj_v, norm_v] + [spatial(c) for c in range(n_chunks)]
        return phases + [out_proj], stage_h2

    def ffn_phases():
        return _ffn_phases(h2_scr, x1_scr, x1_hold_scr, modp_ref, x2_scr, wg_ref, wu_ref, wd_ref)

    def final_norm():
        x2 = x2_scr[...]
        o_ref[...] = x2 * lax.rsqrt(jnp.mean(x2 * x2, axis=-1, keepdims=True) + EPS) * gfin_ref[...]

    @pl.when(i == 0)
    def _():
        phases, stage_h2 = mixer_phases()
        _run(phases + [stage_h2])

    @pl.when(jnp.logical_and(i > 0, i < n_tiles))
    def _():
        phases, stage_h2 = mixer_phases()
        ffn = ffn_phases()
        gaps = [1, 2, 2, 4, 2] + [0] * (n_chunks - 1) + [2]
        _run(_weave(phases, ffn[:-2], gaps) + [stage_h2] + ffn[-2:] + [final_norm])

    @pl.when(i == n_tiles)
    def _():
        _run(ffn_phases() + [final_norm])


def _mix1(x2d, mod4, w, *, tm, mod_row):
    n_tok = x2d.shape[0]
    n_chunks = tm // CHUNK
    n_tiles = n_tok // tm
    tile_of = lambda i, lag: jnp.clip(i - lag, 0, n_tiles - 1)
    in_specs = [
        pl.BlockSpec((tm, D_MODEL), lambda i: (tile_of(i, 0), 0)),
        pl.BlockSpec((None, None, 6, D_MODEL), lambda i: (1, mod_row(tile_of(i, 0)), 0, 0)),
        pl.BlockSpec((None, None, 6, D_MODEL), lambda i: (1, mod_row(tile_of(i, 1)), 0, 0)),
        _resident((1, D_MODEL), layer=1),
        _resident((D_MODEL, 2 * D_MODEL)),
        _resident((1, D_MODEL), layer=0),
        _resident((CMLP_GROUPS, CHUNK, CHUNK), layer=0),
        _resident((CMLP_GROUPS, CHUNK, 1), layer=0),
        _resident((D_MODEL, D_MODEL)),
        _resident((1, D_MODEL), layer=1),
        _resident((D_MODEL, FFN_HIDDEN)),
        _resident((D_MODEL, FFN_HIDDEN)),
        _resident((FFN_HIDDEN, D_MODEL)),
        _resident((1, D_MODEL)),
    ]
    args = [x2d, mod4, mod4, w["norm_mix_g"], w["w_in_c"], w["c_norm_g"], w["w_spatial"], w["b_spatial"],
            w["w_out_c"], w["norm_ffn_g"], w["w_gate1"], w["w_up1"], w["w_down1"], w["final_norm_g"]]
    return pl.pallas_call(
        functools.partial(_mix1_kernel, n_chunks=n_chunks, n_tiles=n_tiles),
        grid=(n_tiles + 1,),
        in_specs=in_specs,
        out_specs=pl.BlockSpec((tm, D_MODEL), lambda i: (tile_of(i, 1), 0)),
        out_shape=jax.ShapeDtypeStruct((n_tok, D_MODEL), F32),
        scratch_shapes=[pltpu.VMEM((tm, D_MODEL), BF16), pltpu.VMEM((tm, D_MODEL), F32),
                        pltpu.VMEM((tm, D_MODEL), BF16), pltpu.VMEM((tm, D_MODEL), F32),
                        pltpu.VMEM((tm, D_MODEL), F32)],
        compiler_params=pltpu.CompilerParams(
            dimension_semantics=("arbitrary",), vmem_limit_bytes=VMEM_LIMIT_BYTES),
        name="chunkmlp_ffn1",
    )(*args)


def _decay_tables(decay_logit):
    lg = jax.nn.log_sigmoid(decay_logit.astype(F32))
    idx = jnp.arange(CHUNK, dtype=F32)
    diff = idx[:, None] - idx[None, :]
    lg_f, lg_b = lg[0], lg[1]
    mask_f = jnp.where(diff >= 0, jnp.exp(jnp.maximum(diff, 0.0) * lg_f[:, None, None]), 0.0)
    mask_b = jnp.where(diff <= 0, jnp.exp(jnp.maximum(-diff, 0.0) * lg_b[:, None, None]), 0.0)
    per_head = lambda t: jnp.repeat(t.T, HEAD_DIM, axis=1)
    lanes = lambda t: jnp.broadcast_to(t[:, None, None], (N_HEADS, 1, HEAD_DIM))
    return {
        "mask": mask_f + mask_b,
        "cross_f": per_head(jnp.exp((idx + 1.0) * lg_f[:, None])),
        "cross_b": per_head(jnp.exp((CHUNK - idx) * lg_b[:, None])),
        "kdec_f": per_head(jnp.exp((CHUNK - 1.0 - idx) * lg_f[:, None])),
        "kdec_b": per_head(jnp.exp(idx * lg_b[:, None])),
        "cd_f": lanes(jnp.exp(CHUNK * lg_f)),
        "cd_b": lanes(jnp.exp(CHUNK * lg_b)),
    }


def _rope_tables(seq_len):
    rows = seq_len // GRID_W
    row = np.repeat(np.arange(rows, dtype=np.float32), GRID_W)
    col = np.tile(np.arange(GRID_W, dtype=np.float32), rows)
    nf = HEAD_DIM // 4
    freqs = np.float32(ROPE_BASE) ** (-np.arange(nf, dtype=np.float32) / np.float32(nf))
    ang = np.concatenate([row[:, None] * freqs, col[:, None] * freqs], axis=-1).astype(np.float64)
    cos, sin = np.cos(ang).astype(np.float32), np.sin(ang).astype(np.float32)
    return {"cos2": jnp.asarray(np.concatenate([cos, cos], axis=-1)),
            "sin2": jnp.asarray(np.concatenate([-sin, sin], axis=-1))}


def _pass_geometry(x, use_rope, mod_row_of_seq):
    n_seq, seq_len, _ = x.shape
    tm = min(TOKEN_TILE, seq_len)
    tiles_per_seq = seq_len // tm
    return dict(n_seq=n_seq, seq_len=seq_len, tm=tm, use_rope=use_rope,
                mod_row=lambda t: mod_row_of_seq(t // tiles_per_seq))


def kernel(x_prompt, x_sample, state_ret, c, c_ctx, ada_w, ada_b, norm_mix_g, norm_ffn_g, w_in_ab,
           ret_decay_logit, ret_norm_g, conv_w, conv_b, w_out_ab, w_in_c, c_norm_g, w_spatial, b_spatial,
           w_out_c, w_gate, w_up, w_down, final_norm_g):
    n_lat = c.shape[0]
    cvecs = jnp.concatenate(
        [c_ctx[None, :], c, jnp.zeros((MOD_ROWS - 1 - n_lat, D_MODEL), F32)], axis=0)
    mod4 = _modulation(cvecs, ada_w, ada_b)

    rows = lambda a: a[..., None, :]
    w = {
        "norm_mix_g": rows(norm_mix_g), "norm_ffn_g": rows(norm_ffn_g),
        "w_in_ab": w_in_ab.astype(BF16), "ret_norm_g": rows(ret_norm_g),
        "conv_w": conv_w, "conv_b": rows(conv_b), "c_norm_g": rows(c_norm_g),
        "w_spatial": w_spatial.astype(BF16), "b_spatial": b_spatial[..., None],
        "final_norm_g": final_norm_g.reshape(1, -1),
    }
    tables = _decay_tables(ret_decay_logit[0])
    lat_tables = dict(tables, **_rope_tables(x_sample.shape[1]))
    ctx = _pass_geometry(x_prompt, False, lambda b: 0)
    lat = _pass_geometry(x_sample, True, lambda b: b + 1)
    flat = lambda x: x.reshape(-1, D_MODEL)
    without_rope = lambda g: {k: v for k, v in g.items() if k != "use_rope"}

    later_weights = {"w_out_ab": (w_out_ab, 0), "w_gate0": (w_gate, 0), "w_up0": (w_up, 0),
                     "w_down0": (w_down, 0), "w_in_c": (w_in_c, 0), "w_out_c": (w_out_c, 0),
                     "w_gate1": (w_gate, 1), "w_up1": (w_up, 1), "w_down1": (w_down, 1)}
    p_lat, dsf_lat, sbp_lat, sbfin_lat, *converted = _inproj(
        flat(x_sample), mod4, w, lat_tables, state_ret, **lat, cast=tuple(later_weights.values()))
    w.update(zip(later_weights, converted))
    p_ctx, dsf_ctx, sbp_ctx, sbfin_ctx = _inproj(flat(x_prompt), mod4, w, tables, None, **ctx)

    x1_lat, _ = _mix0(flat(x_sample), p_lat, dsf_lat, sbp_lat, sbfin_lat, mod4, lat_tables, state_ret, w,
                      **without_rope(lat))
    x1_ctx, state_ctx = _mix0(flat(x_prompt), p_ctx, dsf_ctx, sbp_ctx, sbfin_ctx, mod4, tables, None, w,
                              **without_rope(ctx))

    y_sample = _mix1(x1_lat, mod4, w, tm=lat["tm"], mod_row=lat["mod_row"])
    y_prompt = _mix1(x1_ctx, mod4, w, tm=ctx["tm"], mod_row=ctx["mod_row"])
    return (y_prompt.reshape(x_prompt.shape), y_sample.reshape(x_sample.shape), state_ctx[:, None])
```

```python
import functools
import math

import numpy as np
import jax
import jax.numpy as jnp
from jax import lax
from jax.experimental import pallas as pl
from jax.experimental.pallas import tpu as pltpu

D_MODEL = 1024
N_HEADS = 4
HEAD_DIM = 128
RET_WIDTH = N_HEADS * HEAD_DIM
CONV_WIDTH = 512
CHUNK = 128
CMLP_GROUPS = 4
GROUP_WIDTH = D_MODEL // CMLP_GROUPS
FFN_HIDDEN = 2816
AB_IN = 4 * RET_WIDTH + 3 * CONV_WIDTH
GRID_W = 64
ROPE_BASE = 10000.0
EPS = 1e-6
MOD_ROWS = 16
MOD_COLS_PER_STEP = 1536

P_Q, P_K, P_V, P_G, P_BG, P_CX = 0, 512, 1024, 1536, 2048, 2560
P_COLS = 3072
LANES = 128
BF16_SUBLANES = 16
HALO_ROWS = BF16_SUBLANES

MXU_TILE = 256
TOKEN_TILE = 512
MIX0_FFN_SLAB = 2 * MXU_TILE
MIX1_FFN_SLAB = MXU_TILE
VMEM_LIMIT_BYTES = 56 * 1024 * 1024
F32 = jnp.float32
BF16 = jnp.bfloat16


def _dot(a, b):
    return jnp.dot(a, b, preferred_element_type=F32)


def _resident(shape, layer=None):
    if layer is None:
        zeros = (0,) * len(shape)
        return pl.BlockSpec(shape, lambda *_: zeros, pipeline_mode=pl.Buffered(1))
    index = (layer,) + (0,) * len(shape)
    return pl.BlockSpec((None,) + tuple(shape), lambda *_: index, pipeline_mode=pl.Buffered(1))


def _norm_mod(x, gain, shift, scale):
    y = x * lax.rsqrt(jnp.mean(x * x, axis=-1, keepdims=True) + EPS)
    return (y * gain) * (1.0 + scale) + shift


def _silu(x):
    hx = 0.5 * x
    return hx + hx * jnp.tanh(hx)


def _gelu_tanh(x):
    return 0.5 * x * (1.0 + jnp.tanh(math.sqrt(2.0 / math.pi) * (x + 0.044715 * (x * x * x))))


def _pipelined(first_halves, second_halves):
    order = [first_halves[0]]
    for k in range(1, len(first_halves)):
        order += [first_halves[k], second_halves[k - 1]]
    order.append(second_halves[-1])
    return order


def _weave(primary, filler, gaps):
    filler = list(filler)
    order = []
    for phase, gap in zip(primary, gaps):
        order += filler[:gap] + [phase]
        filler = filler[gap:]
    return order + filler


def _run(phases):
    for phase in phases:
        phase()


def _cast_rider(w3d, layer, n_steps, step_of):
    _, rows, cols = w3d.shape
    for n_col in range(1, cols // LANES + 1):
        n_row, rem = divmod(n_steps, n_col)
        if (not rem and cols % (n_col * LANES) == 0 and rows % n_row == 0
                and (rows // n_row) % BF16_SUBLANES == 0):
            break
    else:
        raise ValueError(f"a {rows} x {cols} weight does not split into {n_steps} bf16-tileable blocks")
    block = (rows // n_row, cols // n_col)
    index = lambda *g: (step_of(*g) // n_col, step_of(*g) % n_col)
    return (pl.BlockSpec((None,) + block, lambda *g: (layer,) + index(*g)),
            pl.BlockSpec(block, index),
            jax.ShapeDtypeStruct((rows, cols), BF16))


def _ffn_phases(h2_scr, x1_scr, x1_hold_scr, gate2_ref, dst_ref, wg_ref, wu_ref, wd_ref, slab_width):
    acts = {}
    bounds = [(lo, min(lo + slab_width, FFN_HIDDEN)) for lo in range(0, FFN_HIDDEN, slab_width)]
    last = len(bounds) - 1

    def up(k):
        def phase():
            lo, hi = bounds[k]
            h2 = h2_scr[...]
            acts[k] = (_silu(_dot(h2, wg_ref[:, lo:hi])) * _dot(h2, wu_ref[:, lo:hi])).astype(BF16)
        return phase

    def down(k):
        def phase():
            lo, hi = bounds[k]
            part = _dot(acts.pop(k), wd_ref[lo:hi, :])
            if k == 0:
                x1_hold_scr[...] = x1_scr[...]
                dst_ref[...] = part
            elif k < last:
                dst_ref[...] += part
            else:
                dst_ref[...] = x1_hold_scr[...] + gate2_ref[5:6] * (dst_ref[...] + part)
        return phase

    slabs = range(len(bounds))
    return _pipelined([up(k) for k in slabs], [down(k) for k in slabs])


def _mod_kernel(cv_ref, w_ref, b_ref, o_ref):
    sc = _silu(cv_ref[...]).astype(BF16)
    o_ref[...] = _dot(sc, w_ref[...].astype(BF16)) + b_ref[...]


def _modulation(cvecs, ada_w, ada_b):
    depth, _, n_mod = ada_w.shape
    steps = n_mod // MOD_COLS_PER_STEP
    out = pl.pallas_call(
        _mod_kernel,
        grid=(depth, steps),
        in_specs=[
            pl.BlockSpec((MOD_ROWS, D_MODEL), lambda l, n: (0, 0)),
            pl.BlockSpec((None, D_MODEL, MOD_COLS_PER_STEP), lambda l, n: (l, 0, n)),
            pl.BlockSpec((None, 1, MOD_COLS_PER_STEP), lambda l, n: (l, 0, n)),
        ],
        out_specs=pl.BlockSpec((None, MOD_ROWS, MOD_COLS_PER_STEP), lambda l, n: (l, 0, n)),
        out_shape=jax.ShapeDtypeStruct((depth, MOD_ROWS, n_mod), F32),
        compiler_params=pltpu.CompilerParams(
            dimension_semantics=("arbitrary", "arbitrary"), vmem_limit_bytes=VMEM_LIMIT_BYTES),
        name="adaln_modulation",
    )(cvecs, ada_w, ada_b.reshape(depth, 1, n_mod))
    return out.reshape(depth, MOD_ROWS, 6, D_MODEL)


def _inproj_kernel(*refs, n_chunks, use_rope, has_init, n_riders):
    it = iter(refs)
    x_ref, mod_ref, gain_ref, w_ref = next(it), next(it), next(it), next(it)
    cos_ref = sin_ref = s0_ref = None
    if use_rope:
        cos_ref, sin_ref = next(it), next(it)
    kdf_ref, kdb_ref, cdb_ref = next(it), next(it), next(it)
    if has_init:
        s0_ref = next(it)
    cast_srcs = [next(it) for _ in range(n_riders)]
    p_ref, dsf_ref, sbp_ref, sb_ref = next(it), next(it), next(it), next(it)
    cast_dsts = [next(it) for _ in range(n_riders)]

    @pl.when(pl.program_id(1) == 0)
    def _():
        if has_init:
            sb_ref[...] = s0_ref[...]
        else:
            sb_ref[...] = jnp.zeros_like(sb_ref)

    for src, dst in zip(cast_srcs, cast_dsts):
        dst[...] = src[...].astype(BF16)

    mod = mod_ref[...]
    h = _norm_mod(x_ref[...], gain_ref[...], mod[0:1], mod[1:2]).astype(BF16)
    p = _dot(h, w_ref[...])

    q = p[:, 0:RET_WIDTH] * (HEAD_DIM ** -0.5)
    k = p[:, RET_WIDTH:2 * RET_WIDTH]
    v = p[:, 2 * RET_WIDTH:3 * RET_WIDTH]
    if use_rope:
        cos2, sin2 = cos_ref[...], sin_ref[...]

        def rope(a):
            heads = []
            for hd in range(N_HEADS):
                ah = a[:, hd * HEAD_DIM:(hd + 1) * HEAD_DIM]
                heads.append(ah * cos2 + pltpu.roll(ah, HEAD_DIM // 2, 1) * sin2)
            return jnp.concatenate(heads, axis=1)

        q, k = rope(q), rope(k)

    p_ref[:, P_Q:P_Q + RET_WIDTH] = q.astype(BF16)
    p_ref[:, P_K:P_K + RET_WIDTH] = k.astype(BF16)
    v16 = v.astype(BF16)
    p_ref[:, P_V:P_V + RET_WIDTH] = v16
    p_ref[:, P_G:P_BG + CONV_WIDTH] = p[:, 3 * RET_WIDTH:4 * RET_WIDTH + CONV_WIDTH].astype(BF16)
    cg = p[:, 4 * RET_WIDTH + CONV_WIDTH:4 * RET_WIDTH + 2 * CONV_WIDTH]
    xc = p[:, 4 * RET_WIDTH + 2 * CONV_WIDTH:AB_IN]
    p_ref[:, P_CX:P_COLS] = (cg * xc).astype(BF16)

    tn = (((0,), (0,)), ((), ()))
    for c in reversed(range(n_chunks)):
        rows = slice(c * CHUNK, (c + 1) * CHUNK)
        kc = k[rows]
        kf = (kc * kdf_ref[...]).astype(BF16)
        kb = (kc * kdb_ref[...]).astype(BF16)
        vc = v16[rows]
        for hd in range(N_HEADS):
            cols = slice(hd * HEAD_DIM, (hd + 1) * HEAD_DIM)
            dsf_ref[c, hd] = lax.dot_general(kf[:, cols], vc[:, cols], tn, preferred_element_type=F32)
            dsb = lax.dot_general(kb[:, cols], vc[:, cols], tn, preferred_element_type=F32)
            sb = sb_ref[hd]
            sbp_ref[c, hd] = sb.astype(BF16)
            sb_ref[hd] = sb * cdb_ref[hd] + dsb


def _state_spec(index_of_seq, direction):
    return pl.BlockSpec((None, None, None, N_HEADS, HEAD_DIM, HEAD_DIM),
                        lambda *g: (index_of_seq(*g), 0, direction, 0, 0, 0))


def _inproj(x2d, mod4, w, tables, state, *, n_seq, seq_len, tm, use_rope, mod_row, cast=()):
    nt = seq_len // tm
    n_chunks = tm // CHUNK
    has_init = state is not None
    tile = lambda b, j: b * nt + (nt - 1 - j)
    riders = [_cast_rider(w3d, layer, n_seq * nt, lambda b, j: b * nt + j) for w3d, layer in cast]

    in_specs = [
        pl.BlockSpec((tm, D_MODEL), lambda b, j: (tile(b, j), 0)),
        pl.BlockSpec((None, None, 6, D_MODEL), lambda b, j: (0, mod_row(tile(b, j)), 0, 0)),
        _resident((1, D_MODEL), layer=0),
        _resident((D_MODEL, AB_IN), layer=0),
    ]
    args = [x2d, mod4, w["norm_mix_g"], w["w_in_ab"]]
    if use_rope:
        in_specs += [pl.BlockSpec((tm, HEAD_DIM), lambda b, j: (nt - 1 - j, 0))] * 2
        args += [tables["cos2"], tables["sin2"]]
    in_specs += [
        pl.BlockSpec((CHUNK, RET_WIDTH), lambda b, j: (0, 0)),
        pl.BlockSpec((CHUNK, RET_WIDTH), lambda b, j: (0, 0)),
        pl.BlockSpec((N_HEADS, 1, HEAD_DIM), lambda b, j: (0, 0, 0)),
    ]
    args += [tables["kdec_f"], tables["kdec_b"], tables["cd_b"]]
    if has_init:
        in_specs.append(_state_spec(lambda b, j: b, 1))
        args.append(state)
    in_specs += [r[0] for r in riders]
    args += [w3d for w3d, _ in cast]

    n_tok = n_seq * seq_len
    chunk_spec = pl.BlockSpec((n_chunks, N_HEADS, HEAD_DIM, HEAD_DIM), lambda b, j: (tile(b, j), 0, 0, 0))
    chunk_shape = (n_tok // CHUNK, N_HEADS, HEAD_DIM, HEAD_DIM)
    return pl.pallas_call(
        functools.partial(_inproj_kernel, n_chunks=n_chunks, use_rope=use_rope, has_init=has_init,
                          n_riders=len(riders)),
        grid=(n_seq, nt),
        in_specs=in_specs,
        out_specs=[
            pl.BlockSpec((tm, P_COLS), lambda b, j: (tile(b, j), 0)),
            chunk_spec,
            chunk_spec,
            pl.BlockSpec((None, N_HEADS, HEAD_DIM, HEAD_DIM), lambda b, j: (b, 0, 0, 0)),
        ] + [r[1] for r in riders],
        out_shape=[
            jax.ShapeDtypeStruct((n_tok, P_COLS), BF16),
            jax.ShapeDtypeStruct(chunk_shape, F32),
            jax.ShapeDtypeStruct(chunk_shape, BF16),
            jax.ShapeDtypeStruct((n_seq, N_HEADS, HEAD_DIM, HEAD_DIM), F32),
        ] + [r[2] for r in riders],
        compiler_params=pltpu.CompilerParams(
            dimension_semantics=("arbitrary", "arbitrary"), vmem_limit_bytes=VMEM_LIMIT_BYTES),
        name="inproj_bwd_scan",
    )(*args)


def _mix0_kernel(*refs, n_chunks, tiles_per_seq, n_tiles, has_init):
    it = iter(refs)
    x_ref, p_ref, hprev_ref, hnext_ref, dsf_ref, sbp_ref, sbfin_ref = (next(it) for _ in range(7))
    s0_ref = next(it) if has_init else None
    (mod_ref, modp_ref, mask_ref, crf_ref, crb_ref, cdf_ref, retg_ref, convw_ref, convb_ref, wout_ref,
     gain_ref, wg_ref, wu_ref, wd_ref) = (next(it) for _ in range(14))
    xo_ref, state_ref = next(it), next(it)
    y_scr, x1_scr, h2_scr, x1_hold_scr = (next(it) for _ in range(4))
    sf_ref = state_ref.at[0]

    i = pl.program_id(0)
    j = lax.rem(i, tiles_per_seq)
    tm = x_ref.shape[0]

    @pl.when(jnp.logical_and(j == 0, i < n_tiles))
    def _():
        state_ref[1] = sbfin_ref[...]
        if has_init:
            sf_ref[...] = s0_ref[...]
        else:
            sf_ref[...] = jnp.zeros_like(sf_ref)

    def mixer_phases():
        nt_dims = (((1,), (1,)), ((), ()))
        live = {}

        def head_cols(base, hd):
            return slice(base + hd * HEAD_DIM, base + (hd + 1) * HEAD_DIM)

        def scores_and_cross(c):
            def phase():
                rows = slice(c * CHUNK, (c + 1) * CHUNK)
                for hd in range(N_HEADS):
                    cols = head_cols(0, hd)
                    qh = p_ref[rows, head_cols(P_Q, hd)]
                    kh = p_ref[rows, head_cols(P_K, hd)]
                    scores = lax.dot_general(qh, kh, nt_dims, preferred_element_type=F32) * mask_ref[hd]
                    sf = sf_ref[hd]
                    states = jnp.concatenate([sf.astype(BF16), sbp_ref[c, hd]], axis=1)
                    qs = _dot(qh, states)
                    cross = qs[:, :HEAD_DIM] * crf_ref[:, cols] + qs[:, HEAD_DIM:] * crb_ref[:, cols]
                    sf_ref[hd] = sf * cdf_ref[hd] + dsf_ref[c, hd]
                    live[c, hd] = (scores.astype(BF16), cross)
            return phase

        def retention_out(c):
            def phase():
                rows = slice(c * CHUNK, (c + 1) * CHUNK)
                for hd in range(N_HEADS):
                    cols = head_cols(0, hd)
                    scores, cross = live.pop((c, hd))
                    o = _dot(scores, p_ref[rows, head_cols(P_V, hd)]) + cross
                    o = o * lax.rsqrt(jnp.mean(o * o, axis=-1, keepdims=True) + EPS) * retg_ref[:, cols]
                    gate = p_ref[rows, head_cols(P_G, hd)].astype(F32)
                    y_scr[rows, cols] = (_silu(gate) * o).astype(BF16)
            return phase

        def conv():
            cx = p_ref[:, P_CX:P_COLS].astype(F32)
            row_id = lax.broadcasted_iota(jnp.int32, cx.shape, 0)
            prev_row = hprev_ref[...].astype(F32)[HALO_ROWS - 1:HALO_ROWS] * jnp.where(j > 0, 1.0, 0.0)
            next_row = hnext_ref[...].astype(F32)[0:1] * jnp.where(j < tiles_per_seq - 1, 1.0, 0.0)
            prev = jnp.where(row_id == 0, prev_row, pltpu.roll(cx, 1, 0))
            nxt = jnp.where(row_id == tm - 1, next_row, pltpu.roll(cx, tm - 1, 0))
            out = convw_ref[0:1] * prev + convw_ref[1:2] * cx + convw_ref[2:3] * nxt + convb_ref[...]
            y_scr[:, RET_WIDTH:] = (p_ref[:, P_BG:P_BG + CONV_WIDTH].astype(F32) * out).astype(BF16)

        def out_proj():
            x1 = x_ref[...] + mod_ref[2:3] * _dot(y_scr[...], wout_ref[...])
            x1_scr[...] = x1
            live["h2"] = _norm_mod(x1, gain_ref[...], mod_ref[3:4], mod_ref[4:5]).astype(BF16)

        def stage_h2():
            h2_scr[...] = live.pop("h2")

        chunks = range(n_chunks)
        retention = _pipelined([scores_and_cross(c) for c in chunks], [retention_out(c) for c in chunks])
        return retention + [conv, out_proj], stage_h2

    def ffn_phases():
        return _ffn_phases(h2_scr, x1_scr, x1_hold_scr, modp_ref, xo_ref, wg_ref, wu_ref, wd_ref,
                           MIX0_FFN_SLAB)

    @pl.when(i == 0)
    def _():
        phases, stage_h2 = mixer_phases()
        _run(phases + [stage_h2])

    @pl.when(jnp.logical_and(i > 0, i < n_tiles))
    def _():
        phases, stage_h2 = mixer_phases()
        ffn = ffn_phases()
        _run(_weave(phases, ffn[:-2], [0] + [1] * (len(phases) - 1)) + [stage_h2] + ffn[-2:])

    @pl.when(i == n_tiles)
    def _():
        _run(ffn_phases())


def _mix0(x2d, p, dsf, sbp, sb_fin, mod4, tables, state, w, *, n_seq, seq_len, tm, mod_row):
    tiles_per_seq = seq_len // tm
    n_chunks = tm // CHUNK
    has_init = state is not None
    n_tok = n_seq * seq_len
    n_tiles = n_tok // tm
    halo_per_tile = tm // HALO_ROWS
    last_halo = n_tok // HALO_ROWS - 1
    cx_block = P_CX // CONV_WIDTH
    cur = lambda i: jnp.minimum(i, n_tiles - 1)
    prev = lambda i: jnp.maximum(i - 1, 0)
    seq = lambda i: cur(i) // tiles_per_seq

    chunk_spec = pl.BlockSpec((n_chunks, N_HEADS, HEAD_DIM, HEAD_DIM), lambda i: (cur(i), 0, 0, 0))
    in_specs = [
        pl.BlockSpec((tm, D_MODEL), lambda i: (cur(i), 0)),
        pl.BlockSpec((tm, P_COLS), lambda i: (cur(i), 0)),
        pl.BlockSpec((HALO_ROWS, CONV_WIDTH),
                     lambda i: (jnp.maximum(cur(i) * halo_per_tile - 1, 0), cx_block)),
        pl.BlockSpec((HALO_ROWS, CONV_WIDTH),
                     lambda i: (jnp.minimum((cur(i) + 1) * halo_per_tile, last_halo), cx_block)),
        chunk_spec,
        chunk_spec,
        pl.BlockSpec((None, N_HEADS, HEAD_DIM, HEAD_DIM), lambda i: (seq(i), 0, 0, 0)),
    ]
    args = [x2d, p, p, p, dsf, sbp, sb_fin]
    if has_init:
        in_specs.append(_state_spec(seq, 0))
        args.append(state)
    in_specs += [
        pl.BlockSpec((None, None, 6, D_MODEL), lambda i: (0, mod_row(cur(i)), 0, 0)),
        pl.BlockSpec((None, None, 6, D_MODEL), lambda i: (0, mod_row(prev(i)), 0, 0)),
        _resident((N_HEADS, CHUNK, CHUNK)),
        _resident((CHUNK, RET_WIDTH)),
        _resident((CHUNK, RET_WIDTH)),
        _resident((N_HEADS, 1, HEAD_DIM)),
        _resident((1, RET_WIDTH), layer=0),
        _resident((3, CONV_WIDTH), layer=0),
        _resident((1, CONV_WIDTH), layer=0),
        _resident((D_MODEL, D_MODEL)),
        _resident((1, D_MODEL), layer=0),
        _resident((D_MODEL, FFN_HIDDEN)),
        _resident((D_MODEL, FFN_HIDDEN)),
        _resident((FFN_HIDDEN, D_MODEL)),
    ]
    args += [mod4, mod4, tables["mask"], tables["cross_f"], tables["cross_b"], tables["cd_f"],
             w["ret_norm_g"], w["conv_w"], w["conv_b"], w["w_out_ab"], w["norm_ffn_g"],
             w["w_gate0"], w["w_up0"], w["w_down0"]]
    return pl.pallas_call(
        functools.partial(_mix0_kernel, n_chunks=n_chunks, tiles_per_seq=tiles_per_seq, n_tiles=n_tiles,
                          has_init=has_init),
        grid=(n_tiles + 1,),
        in_specs=in_specs,
        out_specs=[
            pl.BlockSpec((tm, D_MODEL), lambda i: (prev(i), 0)),
            pl.BlockSpec((None, 2, N_HEADS, HEAD_DIM, HEAD_DIM), lambda i: (seq(i), 0, 0, 0, 0)),
        ],
        out_shape=[
            jax.ShapeDtypeStruct((n_tok, D_MODEL), F32),
            jax.ShapeDtypeStruct((n_seq, 2, N_HEADS, HEAD_DIM, HEAD_DIM), F32),
        ],
        scratch_shapes=[pltpu.VMEM((tm, D_MODEL), BF16), pltpu.VMEM((tm, D_MODEL), F32),
                        pltpu.VMEM((tm, D_MODEL), BF16), pltpu.VMEM((tm, D_MODEL), F32)],
        compiler_params=pltpu.CompilerParams(
            dimension_semantics=("arbitrary",), vmem_limit_bytes=VMEM_LIMIT_BYTES),
        name="retention_conv_ffn0",
    )(*args)


def _mix1_kernel(x_ref, mod_ref, modp_ref, gain_ref, win_ref, vg_ref, ws_ref, bs_ref, wout_ref, gain2_ref,
                 wg_ref, wu_ref, wd_ref, gfin_ref, o_ref, gated_scr, x1_scr, h2_scr, x2_scr, x1_hold_scr, *,
                 n_chunks, n_tiles):
    i = pl.program_id(0)

    def mixer_phases():
        live = {}

        def norm_mix():
            live["h"] = _norm_mod(x_ref[...], gain_ref[...], mod_ref[0:1], mod_ref[1:2]).astype(BF16)

        def proj_u():
            live["u"] = _gelu_tanh(_dot(live["h"], win_ref[:, :D_MODEL]))

        def proj_v():
            live["v"] = _gelu_tanh(_dot(live.pop("h"), win_ref[:, D_MODEL:]))

        def norm_v():
            v = live.pop("v")
            live["v"] = (v * lax.rsqrt(jnp.mean(v * v, axis=-1, keepdims=True) + EPS) * vg_ref[...]).astype(BF16)

        def spatial(c):
            def phase():
                rows = slice(c * CHUNK, (c + 1) * CHUNK)
                for g in range(CMLP_GROUPS):
                    cols = slice(g * GROUP_WIDTH, (g + 1) * GROUP_WIDTH)
                    s = _dot(ws_ref[g], live["v"][rows, cols]) + bs_ref[g]
                    gated_scr[rows, cols] = (live["u"][rows, cols] * s).astype(BF16)
            return phase

        def out_proj():
            x1 = x_ref[...] + mod_ref[2:3] * _dot(gated_scr[...], wout_ref[...])
            x1_scr[...] = x1
            live["h2"] = _norm_mod(x1, gain2_ref[...], mod_ref[3:4], mod_ref[4:5]).astype(BF16)

        def stage_h2():
            h2_scr[...] = live.pop("h2")

        phases = [norm_mix, proj_u, proj_v, norm_v] + [spatial(c) for c in range(n_chunks)]
        return phases + [out_proj], stage_h2

    def ffn_phases():
        return _ffn_phases(h2_scr, x1_scr, x1_hold_scr, modp_ref, x2_scr, wg_ref, wu_ref, wd_ref,
                           MIX1_FFN_SLAB)

    def final_norm():
        x2 = x2_scr[...]
        o_ref[...] = x2 * lax.rsqrt(jnp.mean(x2 * x2, axis=-1, keepdims=True) + EPS) * gfin_ref[...]

    @pl.when(i == 0)
    def _():
        phases, stage_h2 = mixer_phases()
        _run(phases + [stage_h2])

    @pl.when(jnp.logical_and(i > 0, i < n_tiles))
    def _():
        phases, stage_h2 = mixer_phases()
        ffn = ffn_phases()
        gaps = [1, 2, 2, 4, 2] + [0] * (n_chunks - 1) + [2]
        _run(_weave(phases, ffn[:-2], gaps) + [stage_h2] + ffn[-2:] + [final_norm])

    @pl.when(i == n_tiles)
    def _():
        _run(ffn_phases() + [final_norm])


def _mix1(x2d, mod4, w, *, tm, mod_row):
    n_tok = x2d.shape[0]
    n_chunks = tm // CHUNK
    n_tiles = n_tok // tm
    tile_of = lambda i, lag: jnp.clip(i - lag, 0, n_tiles - 1)
    in_specs = [
        pl.BlockSpec((tm, D_MODEL), lambda i: (tile_of(i, 0), 0)),
        pl.BlockSpec((None, None, 6, D_MODEL), lambda i: (1, mod_row(tile_of(i, 0)), 0, 0)),
        pl.BlockSpec((None, None, 6, D_MODEL), lambda i: (1, mod_row(tile_of(i, 1)), 0, 0)),
        _resident((1, D_MODEL), layer=1),
        _resident((D_MODEL, 2 * D_MODEL)),
        _resident((1, D_MODEL), layer=0),
        _resident((CMLP_GROUPS, CHUNK, CHUNK), layer=0),
        _resident((CMLP_GROUPS, CHUNK, 1), layer=0),
        _resident((D_MODEL, D_MODEL)),
        _resident((1, D_MODEL), layer=1),
        _resident((D_MODEL, FFN_HIDDEN)),
        _resident((D_MODEL, FFN_HIDDEN)),
        _resident((FFN_HIDDEN, D_MODEL)),
        _resident((1, D_MODEL)),
    ]
    args = [x2d, mod4, mod4, w["norm_mix_g"], w["w_in_c"], w["c_norm_g"], w["w_spatial"], w["b_spatial"],
            w["w_out_c"], w["norm_ffn_g"], w["w_gate1"], w["w_up1"], w["w_down1"], w["final_norm_g"]]
    return pl.pallas_call(
        functools.partial(_mix1_kernel, n_chunks=n_chunks, n_tiles=n_tiles),
        grid=(n_tiles + 1,),
        in_specs=in_specs,
        out_specs=pl.BlockSpec((tm, D_MODEL), lambda i: (tile_of(i, 1), 0)),
        out_shape=jax.ShapeDtypeStruct((n_tok, D_MODEL), F32),
        scratch_shapes=[pltpu.VMEM((tm, D_MODEL), BF16), pltpu.VMEM((tm, D_MODEL), F32),
                        pltpu.VMEM((tm, D_MODEL), BF16), pltpu.VMEM((tm, D_MODEL), F32),
                        pltpu.VMEM((tm, D_MODEL), F32)],
        compiler_params=pltpu.CompilerParams(
            dimension_semantics=("arbitrary",), vmem_limit_bytes=VMEM_LIMIT_BYTES),
        name="chunkmlp_ffn1",
    )(*args)


def _decay_tables(decay_logit):
    lg = jax.nn.log_sigmoid(decay_logit.astype(F32))
    idx = jnp.arange(CHUNK, dtype=F32)
    diff = idx[:, None] - idx[None, :]
    lg_f, lg_b = lg[0], lg[1]
    mask_f = jnp.where(diff >= 0, jnp.exp(jnp.maximum(diff, 0.0) * lg_f[:, None, None]), 0.0)
    mask_b = jnp.where(diff <= 0, jnp.exp(jnp.maximum(-diff, 0.0) * lg_b[:, None, None]), 0.0)
    per_head = lambda t: jnp.repeat(t.T, HEAD_DIM, axis=1)
    lanes = lambda t: jnp.broadcast_to(t[:, None, None], (N_HEADS, 1, HEAD_DIM))
    return {
        "mask": mask_f + mask_b,
        "cross_f": per_head(jnp.exp((idx + 1.0) * lg_f[:, None])),
        "cross_b": per_head(jnp.exp((CHUNK - idx) * lg_b[:, None])),
        "kdec_f": per_head(jnp.exp((CHUNK - 1.0 - idx) * lg_f[:, None])),
        "kdec_b": per_head(jnp.exp(idx * lg_b[:, None])),
        "cd_f": lanes(jnp.exp(CHUNK * lg_f)),
        "cd_b": lanes(jnp.exp(CHUNK * lg_b)),
    }


def _rope_tables(seq_len):
    rows = seq_len // GRID_W
    row = np.repeat(np.arange(rows, dtype=np.float32), GRID_W)
    col = np.tile(np.arange(GRID_W, dtype=np.float32), rows)
    nf = HEAD_DIM // 4
    freqs = np.float32(ROPE_BASE) ** (-np.arange(nf, dtype=np.float32) / np.float32(nf))
    ang = np.concatenate([row[:, None] * freqs, col[:, None] * freqs], axis=-1).astype(np.float64)
    cos, sin = np.cos(ang).astype(np.float32), np.sin(ang).astype(np.float32)
    return {"cos2": jnp.asarray(np.concatenate([cos, cos], axis=-1)),
            "sin2": jnp.asarray(np.concatenate([-sin, sin], axis=-1))}


def _pass_geometry(x, use_rope, mod_row_of_seq):
    n_seq, seq_len, _ = x.shape
    tm = min(TOKEN_TILE, seq_len)
    tiles_per_seq = seq_len // tm
    return dict(n_seq=n_seq, seq_len=seq_len, tm=tm, use_rope=use_rope,
                mod_row=lambda t: mod_row_of_seq(t // tiles_per_seq))


def kernel(x_prompt, x_sample, state_ret, c, c_ctx, ada_w, ada_b, norm_mix_g, norm_ffn_g, w_in_ab,
           ret_decay_logit, ret_norm_g, conv_w, conv_b, w_out_ab, w_in_c, c_norm_g, w_spatial, b_spatial,
           w_out_c, w_gate, w_up, w_down, final_norm_g):
    n_lat = c.shape[0]
    cvecs = jnp.concatenate(
        [c_ctx[None, :], c, jnp.zeros((MOD_ROWS - 1 - n_lat, D_MODEL), F32)], axis=0)
    mod4 = _modulation(cvecs, ada_w, ada_b)

    rows = lambda a: a[..., None, :]
    w = {
        "norm_mix_g": rows(norm_mix_g), "norm_ffn_g": rows(norm_ffn_g),
        "w_in_ab": w_in_ab.astype(BF16), "ret_norm_g": rows(ret_norm_g),
        "conv_w": conv_w, "conv_b": rows(conv_b), "c_norm_g": rows(c_norm_g),
        "w_spatial": w_spatial.astype(BF16), "b_spatial": b_spatial[..., None],
        "final_norm_g": final_norm_g.reshape(1, -1),
    }
    tables = _decay_tables(ret_decay_logit[0])
    lat_tables = dict(tables, **_rope_tables(x_sample.shape[1]))
    ctx = _pass_geometry(x_prompt, False, lambda b: 0)
    lat = _pass_geometry(x_sample, True, lambda b: b + 1)
    flat = lambda x: x.reshape(-1, D_MODEL)
    without_rope = lambda g: {k: v for k, v in g.items() if k != "use_rope"}

    later_weights = {"w_out_ab": (w_out_ab, 0), "w_gate0": (w_gate, 0), "w_up0": (w_up, 0),
                     "w_down0": (w_down, 0), "w_in_c": (w_in_c, 0), "w_out_c": (w_out_c, 0),
                     "w_gate1": (w_gate, 1), "w_up1": (w_up, 1), "w_down1": (w_down, 1)}
    p_lat, dsf_lat, sbp_lat, sbfin_lat, *converted = _inproj(
        flat(x_sample), mod4, w, lat_tables, state_ret, **lat, cast=tuple(later_weights.values()))
    w.update(zip(later_weights, converted))
    p_ctx, dsf_ctx, sbp_ctx, sbfin_ctx = _inproj(flat(x_prompt), mod4, w, tables, None, **ctx)

    x1_lat, _ = _mix0(flat(x_sample), p_lat, dsf_lat, sbp_lat, sbfin_lat, mod4, lat_tables, state_ret, w,
                      **without_rope(lat))
    x1_ctx, state_ctx = _mix0(flat(x_prompt), p_ctx, dsf_ctx, sbp_ctx, sbfin_ctx, mod4, tables, None, w,
                              **without_rope(ctx))

    y_sample = _mix1(x1_lat, mod4, w, tm=lat["tm"], mod_row=lat["mod_row"])
    y_prompt = _mix1(x1_ctx, mod4, w, tm=TOKEN_TILE, mod_row=lambda t: 0)
    return (y_prompt.reshape(x_prompt.shape), y_sample.reshape(x_sample.shape), state_ctx[:, None])
```

```python
import functools
import math

import numpy as np
import jax
import jax.numpy as jnp
from jax import lax
from jax.experimental import pallas as pl
from jax.experimental.pallas import tpu as pltpu

D_MODEL = 1024
N_HEADS = 4
HEAD_DIM = 128
RET_WIDTH = N_HEADS * HEAD_DIM
CONV_WIDTH = 512
CHUNK = 128
CMLP_GROUPS = 4
GROUP_WIDTH = D_MODEL // CMLP_GROUPS
FFN_HIDDEN = 2816
AB_IN = 4 * RET_WIDTH + 3 * CONV_WIDTH
GRID_W = 64
ROPE_BASE = 10000.0
EPS = 1e-6
MOD_ROWS = 16
MOD_COLS_PER_STEP = 1536

P_Q, P_K, P_V, P_G, P_BG, P_CX = 0, 512, 1024, 1536, 2048, 2560
P_COLS = 3072
LANES = 128
BF16_SUBLANES = 16
HALO_ROWS = BF16_SUBLANES

MXU_TILE = 256
TOKEN_TILE = 512
FFN_SLAB = 2 * MXU_TILE
VMEM_LIMIT_BYTES = 56 * 1024 * 1024
F32 = jnp.float32
BF16 = jnp.bfloat16


def _dot(a, b):
    return jnp.dot(a, b, preferred_element_type=F32)


def _resident(shape, layer=None):
    if layer is None:
        zeros = (0,) * len(shape)
        return pl.BlockSpec(shape, lambda *_: zeros, pipeline_mode=pl.Buffered(1))
    index = (layer,) + (0,) * len(shape)
    return pl.BlockSpec((None,) + tuple(shape), lambda *_: index, pipeline_mode=pl.Buffered(1))


def _norm_mod(x, gain, shift, scale):
    y = x * lax.rsqrt(jnp.mean(x * x, axis=-1, keepdims=True) + EPS)
    return (y * gain) * (1.0 + scale) + shift


def _silu(x):
    hx = 0.5 * x
    return hx + hx * jnp.tanh(hx)


def _gelu_tanh(x):
    return 0.5 * x * (1.0 + jnp.tanh(math.sqrt(2.0 / math.pi) * (x + 0.044715 * (x * x * x))))


def _pipelined(first_halves, second_halves):
    order = [first_halves[0]]
    for k in range(1, len(first_halves)):
        order += [first_halves[k], second_halves[k - 1]]
    order.append(second_halves[-1])
    return order


def _weave(primary, filler, gaps):
    filler = list(filler)
    order = []
    for phase, gap in zip(primary, gaps):
        order += filler[:gap] + [phase]
        filler = filler[gap:]
    return order + filler


def _run(phases):
    for phase in phases:
        phase()


def _cast_rider(w3d, layer, n_steps, step_of):
    _, rows, cols = w3d.shape
    for n_col in range(1, cols // LANES + 1):
        n_row, rem = divmod(n_steps, n_col)
        if (not rem and cols % (n_col * LANES) == 0 and rows % n_row == 0
                and (rows // n_row) % BF16_SUBLANES == 0):
            break
    else:
        raise ValueError(f"a {rows} x {cols} weight does not split into {n_steps} bf16-tileable blocks")
    block = (rows // n_row, cols // n_col)
    index = lambda *g: (step_of(*g) // n_col, step_of(*g) % n_col)
    return (pl.BlockSpec((None,) + block, lambda *g: (layer,) + index(*g)),
            pl.BlockSpec(block, index),
            jax.ShapeDtypeStruct((rows, cols), BF16))


def _ffn_phases(h2_scr, x1_scr, x1_hold_scr, gate2_ref, dst_ref, wg_ref, wu_ref, wd_ref):
    acts = {}
    bounds = [(lo, min(lo + FFN_SLAB, FFN_HIDDEN)) for lo in range(0, FFN_HIDDEN, FFN_SLAB)]
    last = len(bounds) - 1

    def up(k):
        def phase():
            lo, hi = bounds[k]
            h2 = h2_scr[...]
            acts[k] = (_silu(_dot(h2, wg_ref[:, lo:hi])) * _dot(h2, wu_ref[:, lo:hi])).astype(BF16)
        return phase

    def down(k):
        def phase():
            lo, hi = bounds[k]
            part = _dot(acts.pop(k), wd_ref[lo:hi, :])
            if k == 0:
                x1_hold_scr[...] = x1_scr[...]
                dst_ref[...] = part
            elif k < last:
                dst_ref[...] += part
            else:
                dst_ref[...] = x1_hold_scr[...] + gate2_ref[5:6] * (dst_ref[...] + part)
        return phase

    slabs = range(len(bounds))
    return _pipelined([up(k) for k in slabs], [down(k) for k in slabs])


def _mod_kernel(cv_ref, w_ref, b_ref, o_ref):
    sc = _silu(cv_ref[...]).astype(BF16)
    o_ref[...] = _dot(sc, w_ref[...].astype(BF16)) + b_ref[...]


def _modulation(cvecs, ada_w, ada_b):
    depth, _, n_mod = ada_w.shape
    steps = n_mod // MOD_COLS_PER_STEP
    out = pl.pallas_call(
        _mod_kernel,
        grid=(depth, steps),
        in_specs=[
            pl.BlockSpec((MOD_ROWS, D_MODEL), lambda l, n: (0, 0)),
            pl.BlockSpec((None, D_MODEL, MOD_COLS_PER_STEP), lambda l, n: (l, 0, n)),
            pl.BlockSpec((None, 1, MOD_COLS_PER_STEP), lambda l, n: (l, 0, n)),
        ],
        out_specs=pl.BlockSpec((None, MOD_ROWS, MOD_COLS_PER_STEP), lambda l, n: (l, 0, n)),
        out_shape=jax.ShapeDtypeStruct((depth, MOD_ROWS, n_mod), F32),
        compiler_params=pltpu.CompilerParams(
            dimension_semantics=("arbitrary", "arbitrary"), vmem_limit_bytes=VMEM_LIMIT_BYTES),
        name="adaln_modulation",
    )(cvecs, ada_w, ada_b.reshape(depth, 1, n_mod))
    return out.reshape(depth, MOD_ROWS, 6, D_MODEL)


def _inproj_kernel(*refs, n_chunks, use_rope, has_init, n_riders):
    it = iter(refs)
    x_ref, mod_ref, gain_ref, w_ref = next(it), next(it), next(it), next(it)
    cos_ref = sin_ref = s0_ref = None
    if use_rope:
        cos_ref, sin_ref = next(it), next(it)
    kdf_ref, kdb_ref, cdb_ref = next(it), next(it), next(it)
    if has_init:
        s0_ref = next(it)
    cast_srcs = [next(it) for _ in range(n_riders)]
    p_ref, dsf_ref, sbp_ref, sb_ref = next(it), next(it), next(it), next(it)
    cast_dsts = [next(it) for _ in range(n_riders)]

    @pl.when(pl.program_id(1) == 0)
    def _():
        if has_init:
            sb_ref[...] = s0_ref[...]
        else:
            sb_ref[...] = jnp.zeros_like(sb_ref)

    for src, dst in zip(cast_srcs, cast_dsts):
        dst[...] = src[...].astype(BF16)

    mod = mod_ref[...]
    h = _norm_mod(x_ref[...], gain_ref[...], mod[0:1], mod[1:2]).astype(BF16)
    p = _dot(h, w_ref[...])

    q = p[:, 0:RET_WIDTH] * (HEAD_DIM ** -0.5)
    k = p[:, RET_WIDTH:2 * RET_WIDTH]
    v = p[:, 2 * RET_WIDTH:3 * RET_WIDTH]
    if use_rope:
        cos2, sin2 = cos_ref[...], sin_ref[...]

        def rope(a):
            heads = []
            for hd in range(N_HEADS):
                ah = a[:, hd * HEAD_DIM:(hd + 1) * HEAD_DIM]
                heads.append(ah * cos2 + pltpu.roll(ah, HEAD_DIM // 2, 1) * sin2)
            return jnp.concatenate(heads, axis=1)

        q, k = rope(q), rope(k)

    p_ref[:, P_Q:P_Q + RET_WIDTH] = q.astype(BF16)
    p_ref[:, P_K:P_K + RET_WIDTH] = k.astype(BF16)
    v16 = v.astype(BF16)
    p_ref[:, P_V:P_V + RET_WIDTH] = v16
    p_ref[:, P_G:P_BG + CONV_WIDTH] = p[:, 3 * RET_WIDTH:4 * RET_WIDTH + CONV_WIDTH].astype(BF16)
    cg = p[:, 4 * RET_WIDTH + CONV_WIDTH:4 * RET_WIDTH + 2 * CONV_WIDTH]
    xc = p[:, 4 * RET_WIDTH + 2 * CONV_WIDTH:AB_IN]
    p_ref[:, P_CX:P_COLS] = (cg * xc).astype(BF16)

    tn = (((0,), (0,)), ((), ()))
    for c in reversed(range(n_chunks)):
        rows = slice(c * CHUNK, (c + 1) * CHUNK)
        kc = k[rows]
        kf = (kc * kdf_ref[...]).astype(BF16)
        kb = (kc * kdb_ref[...]).astype(BF16)
        vc = v16[rows]
        for hd in range(N_HEADS):
            cols = slice(hd * HEAD_DIM, (hd + 1) * HEAD_DIM)
            dsf_ref[c, hd] = lax.dot_general(kf[:, cols], vc[:, cols], tn, preferred_element_type=F32)
            dsb = lax.dot_general(kb[:, cols], vc[:, cols], tn, preferred_element_type=F32)
            sb = sb_ref[hd]
            sbp_ref[c, hd] = sb.astype(BF16)
            sb_ref[hd] = sb * cdb_ref[hd] + dsb


def _state_spec(index_of_seq, direction):
    return pl.BlockSpec((None, None, None, N_HEADS, HEAD_DIM, HEAD_DIM),
                        lambda *g: (index_of_seq(*g), 0, direction, 0, 0, 0))


def _inproj(x2d, mod4, w, tables, state, *, n_seq, seq_len, tm, use_rope, mod_row, cast=()):
    nt = seq_len // tm
    n_chunks = tm // CHUNK
    has_init = state is not None
    tile = lambda b, j: b * nt + (nt - 1 - j)
    riders = [_cast_rider(w3d, layer, n_seq * nt, lambda b, j: b * nt + j) for w3d, layer in cast]

    in_specs = [
        pl.BlockSpec((tm, D_MODEL), lambda b, j: (tile(b, j), 0)),
        pl.BlockSpec((None, None, 6, D_MODEL), lambda b, j: (0, mod_row(tile(b, j)), 0, 0)),
        _resident((1, D_MODEL), layer=0),
        _resident((D_MODEL, AB_IN), layer=0),
    ]
    args = [x2d, mod4, w["norm_mix_g"], w["w_in_ab"]]
    if use_rope:
        in_specs += [pl.BlockSpec((tm, HEAD_DIM), lambda b, j: (nt - 1 - j, 0))] * 2
        args += [tables["cos2"], tables["sin2"]]
    in_specs += [
        pl.BlockSpec((CHUNK, RET_WIDTH), lambda b, j: (0, 0)),
        pl.BlockSpec((CHUNK, RET_WIDTH), lambda b, j: (0, 0)),
        pl.BlockSpec((N_HEADS, 1, HEAD_DIM), lambda b, j: (0, 0, 0)),
    ]
    args += [tables["kdec_f"], tables["kdec_b"], tables["cd_b"]]
    if has_init:
        in_specs.append(_state_spec(lambda b, j: b, 1))
        args.append(state)
    in_specs += [r[0] for r in riders]
    args += [w3d for w3d, _ in cast]

    n_tok = n_seq * seq_len
    chunk_spec = pl.BlockSpec((n_chunks, N_HEADS, HEAD_DIM, HEAD_DIM), lambda b, j: (tile(b, j), 0, 0, 0))
    chunk_shape = (n_tok // CHUNK, N_HEADS, HEAD_DIM, HEAD_DIM)
    return pl.pallas_call(
        functools.partial(_inproj_kernel, n_chunks=n_chunks, use_rope=use_rope, has_init=has_init,
                          n_riders=len(riders)),
        grid=(n_seq, nt),
        in_specs=in_specs,
        out_specs=[
            pl.BlockSpec((tm, P_COLS), lambda b, j: (tile(b, j), 0)),
            chunk_spec,
            chunk_spec,
            pl.BlockSpec((None, N_HEADS, HEAD_DIM, HEAD_DIM), lambda b, j: (b, 0, 0, 0)),
        ] + [r[1] for r in riders],
        out_shape=[
            jax.ShapeDtypeStruct((n_tok, P_COLS), BF16),
            jax.ShapeDtypeStruct(chunk_shape, F32),
            jax.ShapeDtypeStruct(chunk_shape, BF16),
            jax.ShapeDtypeStruct((n_seq, N_HEADS, HEAD_DIM, HEAD_DIM), F32),
        ] + [r[2] for r in riders],
        compiler_params=pltpu.CompilerParams(
            dimension_semantics=("arbitrary", "arbitrary"), vmem_limit_bytes=VMEM_LIMIT_BYTES),
        name="inproj_bwd_scan",
    )(*args)


def _mix0_kernel(*refs, n_chunks, tiles_per_seq, n_tiles, has_init):
    it = iter(refs)
    x_ref, p_ref, hprev_ref, hnext_ref, dsf_ref, sbp_ref, sbfin_ref = (next(it) for _ in range(7))
    s0_ref = next(it) if has_init else None
    (mod_ref, modp_ref, mask_ref, crf_ref, crb_ref, cdf_ref, retg_ref, convw_ref, convb_ref, wout_ref,
     gain_ref, wg_ref, wu_ref, wd_ref) = (next(it) for _ in range(14))
    xo_ref, state_ref = next(it), next(it)
    y_scr, x1_scr, h2_scr, x1_hold_scr = (next(it) for _ in range(4))
    sf_ref = state_ref.at[0]

    i = pl.program_id(0)
    j = lax.rem(i, tiles_per_seq)
    tm = x_ref.shape[0]

    @pl.when(jnp.logical_and(j == 0, i < n_tiles))
    def _():
        state_ref[1] = sbfin_ref[...]
        if has_init:
            sf_ref[...] = s0_ref[...]
        else:
            sf_ref[...] = jnp.zeros_like(sf_ref)

    def mixer_phases():
        nt_dims = (((1,), (1,)), ((), ()))
        live = {}

        def head_cols(base, hd):
            return slice(base + hd * HEAD_DIM, base + (hd + 1) * HEAD_DIM)

        def scores_and_cross(c):
            def phase():
                rows = slice(c * CHUNK, (c + 1) * CHUNK)
                for hd in range(N_HEADS):
                    cols = head_cols(0, hd)
                    qh = p_ref[rows, head_cols(P_Q, hd)]
                    kh = p_ref[rows, head_cols(P_K, hd)]
                    scores = lax.dot_general(qh, kh, nt_dims, preferred_element_type=F32) * mask_ref[hd]
                    sf = sf_ref[hd]
                    states = jnp.concatenate([sf.astype(BF16), sbp_ref[c, hd]], axis=1)
                    qs = _dot(qh, states)
                    cross = qs[:, :HEAD_DIM] * crf_ref[:, cols] + qs[:, HEAD_DIM:] * crb_ref[:, cols]
                    sf_ref[hd] = sf * cdf_ref[hd] + dsf_ref[c, hd]
                    live[c, hd] = (scores.astype(BF16), cross)
            return phase

        def retention_out(c):
            def phase():
                rows = slice(c * CHUNK, (c + 1) * CHUNK)
                for hd in range(N_HEADS):
                    cols = head_cols(0, hd)
                    scores, cross = live.pop((c, hd))
                    o = _dot(scores, p_ref[rows, head_cols(P_V, hd)]) + cross
                    o = o * lax.rsqrt(jnp.mean(o * o, axis=-1, keepdims=True) + EPS) * retg_ref[:, cols]
                    gate = p_ref[rows, head_cols(P_G, hd)].astype(F32)
                    y_scr[rows, cols] = (_silu(gate) * o).astype(BF16)
            return phase

        def conv():
            cx = p_ref[:, P_CX:P_COLS].astype(F32)
            row_id = lax.broadcasted_iota(jnp.int32, cx.shape, 0)
            prev_row = hprev_ref[...].astype(F32)[HALO_ROWS - 1:HALO_ROWS] * jnp.where(j > 0, 1.0, 0.0)
            next_row = hnext_ref[...].astype(F32)[0:1] * jnp.where(j < tiles_per_seq - 1, 1.0, 0.0)
            prev = jnp.where(row_id == 0, prev_row, pltpu.roll(cx, 1, 0))
            nxt = jnp.where(row_id == tm - 1, next_row, pltpu.roll(cx, tm - 1, 0))
            out = convw_ref[0:1] * prev + convw_ref[1:2] * cx + convw_ref[2:3] * nxt + convb_ref[...]
            y_scr[:, RET_WIDTH:] = (p_ref[:, P_BG:P_BG + CONV_WIDTH].astype(F32) * out).astype(BF16)

        def out_proj():
            x1 = x_ref[...] + mod_ref[2:3] * _dot(y_scr[...], wout_ref[...])
            x1_scr[...] = x1
            live["h2"] = _norm_mod(x1, gain_ref[...], mod_ref[3:4], mod_ref[4:5]).astype(BF16)

        def stage_h2():
            h2_scr[...] = live.pop("h2")

        chunks = range(n_chunks)
        retention = _pipelined([scores_and_cross(c) for c in chunks], [retention_out(c) for c in chunks])
        return retention + [conv, out_proj], stage_h2

    def ffn_phases():
        return _ffn_phases(h2_scr, x1_scr, x1_hold_scr, modp_ref, xo_ref, wg_ref, wu_ref, wd_ref)

    @pl.when(i == 0)
    def _():
        phases, stage_h2 = mixer_phases()
        _run(phases + [stage_h2])

    @pl.when(jnp.logical_and(i > 0, i < n_tiles))
    def _():
        phases, stage_h2 = mixer_phases()
        ffn = ffn_phases()
        _run(_weave(phases, ffn[:-2], [0] + [1] * (len(phases) - 1)) + [stage_h2] + ffn[-2:])

    @pl.when(i == n_tiles)
    def _():
        _run(ffn_phases())


def _mix0(x2d, p, dsf, sbp, sb_fin, mod4, tables, state, w, *, n_seq, seq_len, tm, mod_row):
    tiles_per_seq = seq_len // tm
    n_chunks = tm // CHUNK
    has_init = state is not None
    n_tok = n_seq * seq_len
    n_tiles = n_tok // tm
    halo_per_tile = tm // HALO_ROWS
    last_halo = n_tok // HALO_ROWS - 1
    cx_block = P_CX // CONV_WIDTH
    cur = lambda i: jnp.minimum(i, n_tiles - 1)
    prev = lambda i: jnp.maximum(i - 1, 0)
    seq = lambda i: cur(i) // tiles_per_seq

    chunk_spec = pl.BlockSpec((n_chunks, N_HEADS, HEAD_DIM, HEAD_DIM), lambda i: (cur(i), 0, 0, 0))
    in_specs = [
        pl.BlockSpec((tm, D_MODEL), lambda i: (cur(i), 0)),
        pl.BlockSpec((tm, P_COLS), lambda i: (cur(i), 0)),
        pl.BlockSpec((HALO_ROWS, CONV_WIDTH),
                     lambda i: (jnp.maximum(cur(i) * halo_per_tile - 1, 0), cx_block)),
        pl.BlockSpec((HALO_ROWS, CONV_WIDTH),
                     lambda i: (jnp.minimum((cur(i) + 1) * halo_per_tile, last_halo), cx_block)),
        chunk_spec,
        chunk_spec,
        pl.BlockSpec((None, N_HEADS, HEAD_DIM, HEAD_DIM), lambda i: (seq(i), 0, 0, 0)),
    ]
    args = [x2d, p, p, p, dsf, sbp, sb_fin]
    if has_init:
        in_specs.append(_state_spec(seq, 0))
        args.append(state)
    in_specs += [
        pl.BlockSpec((None, None, 6, D_MODEL), lambda i: (0, mod_row(cur(i)), 0, 0)),
        pl.BlockSpec((None, None, 6, D_MODEL), lambda i: (0, mod_row(prev(i)), 0, 0)),
        _resident((N_HEADS, CHUNK, CHUNK)),
        _resident((CHUNK, RET_WIDTH)),
        _resident((CHUNK, RET_WIDTH)),
        _resident((N_HEADS, 1, HEAD_DIM)),
        _resident((1, RET_WIDTH), layer=0),
        _resident((3, CONV_WIDTH), layer=0),
        _resident((1, CONV_WIDTH), layer=0),
        _resident((D_MODEL, D_MODEL)),
        _resident((1, D_MODEL), layer=0),
        _resident((D_MODEL, FFN_HIDDEN)),
        _resident((D_MODEL, FFN_HIDDEN)),
        _resident((FFN_HIDDEN, D_MODEL)),
    ]
    args += [mod4, mod4, tables["mask"], tables["cross_f"], tables["cross_b"], tables["cd_f"],
             w["ret_norm_g"], w["conv_w"], w["conv_b"], w["w_out_ab"], w["norm_ffn_g"],
             w["w_gate0"], w["w_up0"], w["w_down0"]]
    return pl.pallas_call(
        functools.partial(_mix0_kernel, n_chunks=n_chunks, tiles_per_seq=tiles_per_seq, n_tiles=n_tiles,
                          has_init=has_init),
        grid=(n_tiles + 1,),
        in_specs=in_specs,
        out_specs=[
            pl.BlockSpec((tm, D_MODEL), lambda i: (prev(i), 0)),
            pl.BlockSpec((None, 2, N_HEADS, HEAD_DIM, HEAD_DIM), lambda i: (seq(i), 0, 0, 0, 0)),
        ],
        out_shape=[
            jax.ShapeDtypeStruct((n_tok, D_MODEL), F32),
            jax.ShapeDtypeStruct((n_seq, 2, N_HEADS, HEAD_DIM, HEAD_DIM), F32),
        ],
        scratch_shapes=[pltpu.VMEM((tm, D_MODEL), BF16), pltpu.VMEM((tm, D_MODEL), F32),
                        pltpu.VMEM((tm, D_MODEL), BF16), pltpu.VMEM((tm, D_MODEL), F32)],
        compiler_params=pltpu.CompilerParams(
            dimension_semantics=("arbitrary",), vmem_limit_bytes=VMEM_LIMIT_BYTES),
        name="retention_conv_ffn0",
    )(*args)


def _mix1_kernel(x_ref, mod_ref, modp_ref, gain_ref, win_ref, vg_ref, ws_ref, bs_ref, wout_ref, gain2_ref,
                 wg_ref, wu_ref, wd_ref, gfin_ref, o_ref, gated_scr, x1_scr, h2_scr, x2_scr, x1_hold_scr, *,
                 n_chunks, n_tiles):
    i = pl.program_id(0)

    def mixer_phases():
        live = {}

        def norm_mix():
            live["h"] = _norm_mod(x_ref[...], gain_ref[...], mod_ref[0:1], mod_ref[1:2]).astype(BF16)

        def proj_u():
            live["u"] = _gelu_tanh(_dot(live["h"], win_ref[:, :D_MODEL]))

        def proj_v():
            live["v"] = _gelu_tanh(_dot(live.pop("h"), win_ref[:, D_MODEL:]))

        def norm_v():
            v = live.pop("v")
            live["v"] = (v * lax.rsqrt(jnp.mean(v * v, axis=-1, keepdims=True) + EPS) * vg_ref[...]).astype(BF16)

        def spatial(c):
            def phase():
                rows = slice(c * CHUNK, (c + 1) * CHUNK)
                for g in range(CMLP_GROUPS):
                    cols = slice(g * GROUP_WIDTH, (g + 1) * GROUP_WIDTH)
                    s = _dot(ws_ref[g], live["v"][rows, cols]) + bs_ref[g]
                    gated_scr[rows, cols] = (live["u"][rows, cols] * s).astype(BF16)
            return phase

        def out_proj():
            x1 = x_ref[...] + mod_ref[2:3] * _dot(gated_scr[...], wout_ref[...])
            x1_scr[...] = x1
            live["h2"] = _norm_mod(x1, gain2_ref[...], mod_ref[3:4], mod_ref[4:5]).astype(BF16)

        def stage_h2():
            h2_scr[...] = live.pop("h2")

        phases = [norm_mix, proj_u, proj_v, norm_v] + [spatial(c) for c in range(n_chunks)]
        return phases + [out_proj], stage_h2

    def ffn_phases():
        return _ffn_phases(h2_scr, x1_scr, x1_hold_scr, modp_ref, x2_scr, wg_ref, wu_ref, wd_ref)

    def final_norm():
        x2 = x2_scr[...]
        o_ref[...] = x2 * lax.rsqrt(jnp.mean(x2 * x2, axis=-1, keepdims=True) + EPS) * gfin_ref[...]

    @pl.when(i == 0)
    def _():
        phases, stage_h2 = mixer_phases()
        _run(phases + [stage_h2])

    @pl.when(jnp.logical_and(i > 0, i < n_tiles))
    def _():
        phases, stage_h2 = mixer_phases()
        ffn = ffn_phases()
        gaps = [1, 1, 1, 2, 1] + [0] * (n_chunks - 1) + [1]
        _run(_weave(phases, ffn[:-2], gaps) + [stage_h2] + ffn[-2:] + [final_norm])

    @pl.when(i == n_tiles)
    def _():
        _run(ffn_phases() + [final_norm])


def _mix1(x2d, mod4, w, *, tm, mod_row):
    n_tok = x2d.shape[0]
    n_chunks = tm // CHUNK
    n_tiles = n_tok // tm
    tile_of = lambda i, lag: jnp.clip(i - lag, 0, n_tiles - 1)
    in_specs = [
        pl.BlockSpec((tm, D_MODEL), lambda i: (tile_of(i, 0), 0)),
        pl.BlockSpec((None, None, 6, D_MODEL), lambda i: (1, mod_row(tile_of(i, 0)), 0, 0)),
        pl.BlockSpec((None, None, 6, D_MODEL), lambda i: (1, mod_row(tile_of(i, 1)), 0, 0)),
        _resident((1, D_MODEL), layer=1),
        _resident((D_MODEL, 2 * D_MODEL)),
        _resident((1, D_MODEL), layer=0),
        _resident((CMLP_GROUPS, CHUNK, CHUNK), layer=0),
        _resident((CMLP_GROUPS, CHUNK, 1), layer=0),
        _resident((D_MODEL, D_MODEL)),
        _resident((1, D_MODEL), layer=1),
        _resident((D_MODEL, FFN_HIDDEN)),
        _resident((D_MODEL, FFN_HIDDEN)),
        _resident((FFN_HIDDEN, D_MODEL)),
        _resident((1, D_MODEL)),
    ]
    args = [x2d, mod4, mod4, w["norm_mix_g"], w["w_in_c"], w["c_norm_g"], w["w_spatial"], w["b_spatial"],
            w["w_out_c"], w["norm_ffn_g"], w["w_gate1"], w["w_up1"], w["w_down1"], w["final_norm_g"]]
    return pl.pallas_call(
        functools.partial(_mix1_kernel, n_chunks=n_chunks, n_tiles=n_tiles),
        grid=(n_tiles + 1,),
        in_specs=in_specs,
        out_specs=pl.BlockSpec((tm, D_MODEL), lambda i: (tile_of(i, 1), 0)),
        out_shape=jax.ShapeDtypeStruct((n_tok, D_MODEL), F32),
        scratch_shapes=[pltpu.VMEM((tm, D_MODEL), BF16), pltpu.VMEM((tm, D_MODEL), F32),
                        pltpu.VMEM((tm, D_MODEL), BF16), pltpu.VMEM((tm, D_MODEL), F32),
                        pltpu.VMEM((tm, D_MODEL), F32)],
        compiler_params=pltpu.CompilerParams(
            dimension_semantics=("arbitrary",), vmem_limit_bytes=VMEM_LIMIT_BYTES),
        name="chunkmlp_ffn1",
    )(*args)


def _decay_tables(decay_logit):
    lg = jax.nn.log_sigmoid(decay_logit.astype(F32))
    idx = jnp.arange(CHUNK, dtype=F32)
    diff = idx[:, None] - idx[None, :]
    lg_f, lg_b = lg[0], lg[1]
    mask_f = jnp.where(diff >= 0, jnp.exp(jnp.maximum(diff, 0.0) * lg_f[:, None, None]), 0.0)
    mask_b = jnp.where(diff <= 0, jnp.exp(jnp.maximum(-diff, 0.0) * lg_b[:, None, None]), 0.0)
    per_head = lambda t: jnp.repeat(t.T, HEAD_DIM, axis=1)
    lanes = lambda t: jnp.broadcast_to(t[:, None, None], (N_HEADS, 1, HEAD_DIM))
    return {
        "mask": mask_f + mask_b,
        "cross_f": per_head(jnp.exp((idx + 1.0) * lg_f[:, None])),
        "cross_b": per_head(jnp.exp((CHUNK - idx) * lg_b[:, None])),
        "kdec_f": per_head(jnp.exp((CHUNK - 1.0 - idx) * lg_f[:, None])),
        "kdec_b": per_head(jnp.exp(idx * lg_b[:, None])),
        "cd_f": lanes(jnp.exp(CHUNK * lg_f)),
        "cd_b": lanes(jnp.exp(CHUNK * lg_b)),
    }


def _rope_tables(seq_len):
    rows = seq_len // GRID_W
    row = np.repeat(np.arange(rows, dtype=np.float32), GRID_W)
    col = np.tile(np.arange(GRID_W, dtype=np.float32), rows)
    nf = HEAD_DIM // 4
    freqs = np.float32(ROPE_BASE) ** (-np.arange(nf, dtype=np.float32) / np.float32(nf))
    ang = np.concatenate([row[:, None] * freqs, col[:, None] * freqs], axis=-1).astype(np.float64)
    cos, sin = np.cos(ang).astype(np.float32), np.sin(ang).astype(np.float32)
    return {"cos2": jnp.asarray(np.concatenate([cos, cos], axis=-1)),
            "sin2": jnp.asarray(np.concatenate([-sin, sin], axis=-1))}


def _pass_geometry(x, use_rope, mod_row_of_seq):
    n_seq, seq_len, _ = x.shape
    tm = min(TOKEN_TILE, seq_len)
    tiles_per_seq = seq_len // tm
    return dict(n_seq=n_seq, seq_len=seq_len, tm=tm, use_rope=use_rope,
                mod_row=lambda t: mod_row_of_seq(t // tiles_per_seq))


def kernel(x_prompt, x_sample, state_ret, c, c_ctx, ada_w, ada_b, norm_mix_g, norm_ffn_g, w_in_ab,
           ret_decay_logit, ret_norm_g, conv_w, conv_b, w_out_ab, w_in_c, c_norm_g, w_spatial, b_spatial,
           w_out_c, w_gate, w_up, w_down, final_norm_g):
    n_lat = c.shape[0]
    cvecs = jnp.concatenate(
        [c_ctx[None, :], c, jnp.zeros((MOD_ROWS - 1 - n_lat, D_MODEL), F32)], axis=0)
    mod4 = _modulation(cvecs, ada_w, ada_b)

    rows = lambda a: a[..., None, :]
    w = {
        "norm_mix_g": rows(norm_mix_g), "norm_ffn_g": rows(norm_ffn_g),
        "w_in_ab": w_in_ab.astype(BF16), "ret_norm_g": rows(ret_norm_g),
        "conv_w": conv_w, "conv_b": rows(conv_b), "c_norm_g": rows(c_norm_g),
        "w_spatial": w_spatial.astype(BF16), "b_spatial": b_spatial[..., None],
        "final_norm_g": final_norm_g.reshape(1, -1),
    }
    tables = _decay_tables(ret_decay_logit[0])
    lat_tables = dict(tables, **_rope_tables(x_sample.shape[1]))
    ctx = _pass_geometry(x_prompt, False, lambda b: 0)
    lat = _pass_geometry(x_sample, True, lambda b: b + 1)
    flat = lambda x: x.reshape(-1, D_MODEL)
    without_rope = lambda g: {k: v for k, v in g.items() if k != "use_rope"}

    later_weights = {"w_out_ab": (w_out_ab, 0), "w_gate0": (w_gate, 0), "w_up0": (w_up, 0),
                     "w_down0": (w_down, 0), "w_in_c": (w_in_c, 0), "w_out_c": (w_out_c, 0),
                     "w_gate1": (w_gate, 1), "w_up1": (w_up, 1), "w_down1": (w_down, 1)}
    p_lat, dsf_lat, sbp_lat, sbfin_lat, *converted = _inproj(
        flat(x_sample), mod4, w, lat_tables, state_ret, **lat, cast=tuple(later_weights.values()))
    w.update(zip(later_weights, converted))
    p_ctx, dsf_ctx, sbp_ctx, sbfin_ctx = _inproj(flat(x_prompt), mod4, w, tables, None, **ctx)

    x1_lat, _ = _mix0(flat(x_sample), p_lat, dsf_lat, sbp_lat, sbfin_lat, mod4, lat_tables, state_ret, w,
                      **without_rope(lat))
    x1_ctx, state_ctx = _mix0(flat(x_prompt), p_ctx, dsf_ctx, sbp_ctx, sbfin_ctx, mod4, tables, None, w,
                              **without_rope(ctx))

    y_sample = _mix1(x1_lat, mod4, w, tm=lat["tm"], mod_row=lat["mod_row"])
    y_prompt = _mix1(x1_ctx, mod4, w, tm=TOKEN_TILE, mod_row=lambda t: 0)
    return (y_prompt.reshape(x_prompt.shape), y_sample.reshape(x_sample.shape), state_ctx[:, None])
```
